```python
import jax, jax.numpy as jnp
from jax import lax
import numpy as np

D_MODEL = 4096
BATCH = 1
SEQ = 16384
DEPTH = 4

GRID_W = 64
CTX_LEN = 256
HEAD_DIM = 128
N_HEADS_TOTAL = D_MODEL // HEAD_DIM
N_FOURIER = N_HEADS_TOTAL // 4
N_GLA = 3 * N_HEADS_TOTAL // 8
N_NAT = N_HEADS_TOTAL - N_FOURIER - N_GLA
D_FOURIER = N_FOURIER * HEAD_DIM
D_GLA = N_GLA * HEAD_DIM
GLA_DK = HEAD_DIM // 2
D_GLA_K = N_GLA * GLA_DK
GLA_RANK = 16
GLA_TAU = 16.0
GLA_CHUNK = 64
D_NAT = N_NAT * HEAD_DIM
D_MIX = D_FOURIER + D_GLA + D_NAT
NAT_KR = 8
NAT_KC = 16
ROPE_BASE = 10000.0
D_FF = 11 * D_MODEL // 8
N_MOD = 6
EPS = 1e-6
NEG = -1e30
PROJ_SIZES = (D_FOURIER, D_GLA_K, D_GLA_K, D_GLA, D_GLA, GLA_RANK, GLA_RANK, D_NAT, D_NAT, D_NAT)
D_IN = sum(PROJ_SIZES)

kernel_name = 'hybrid_fourier_gla_natten_dit'


def rmsnorm(x, gain):
    xf = x.astype(jnp.float32)
    y = xf * lax.rsqrt(jnp.mean(xf * xf, axis=-1, keepdims=True) + EPS)
    return (y * gain.astype(jnp.float32)).astype(x.dtype)


def modulate(x, gain, shift, scale):
    return rmsnorm(x, gain) * (1 + scale) + shift


def split_proj(p):
    return jnp.split(p, [int(i) for i in np.cumsum(PROJ_SIZES)[:-1]], axis=-1)


def rope_1d(x, pos):
    half = x.shape[-1] // 2
    freqs = ROPE_BASE ** (-jnp.arange(half, dtype=jnp.float32) / half)
    ang = pos.astype(jnp.float32)[:, None] * freqs
    cos, sin = jnp.cos(ang)[:, None, :], jnp.sin(ang)[:, None, :]
    x1, x2 = x[..., :half].astype(jnp.float32), x[..., half:].astype(jnp.float32)
    return jnp.concatenate([x1 * cos - x2 * sin, x1 * sin + x2 * cos], axis=-1).astype(x.dtype)


def axial_rope(x, prow, pcol):
    half = x.shape[-1] // 2
    return jnp.concatenate([rope_1d(x[..., :half], prow), rope_1d(x[..., half:], pcol)], axis=-1)


def fourier_mix(u, w):
    B, L, _ = u.shape
    ug = u.reshape(B, L, N_FOURIER, HEAD_DIM).astype(jnp.float32)
    f = jnp.fft.fft2(ug, axes=(1, 3), norm='ortho').real.astype(u.dtype)
    return jnp.einsum('blgc,gce->blge', f, w).reshape(B, L, D_FOURIER)


def gla_inputs(q, k, v, a_f, a_b, w_dec, b_dec):
    B, L, _ = q.shape
    def log_decay(a, w, b):
        z = (a @ w + b).astype(jnp.float32)
        return (jax.nn.log_sigmoid(z) / GLA_TAU).reshape(B, L, N_GLA, GLA_DK)
    return (q.reshape(B, L, N_GLA, GLA_DK), k.reshape(B, L, N_GLA, GLA_DK), v.reshape(B, L, N_GLA, HEAD_DIM),
            log_decay(a_f, w_dec[0], b_dec[0]), log_decay(a_b, w_dec[1], b_dec[1]))


def gla_scan(q, k, v, log_a, s0):
    B, L, H, dk = q.shape
    dv = v.shape[-1]
    n = L // GLA_CHUNK
    def chunks(t):
        return t.reshape(B, n, GLA_CHUNK, H, t.shape[-1]).transpose(0, 3, 1, 2, 4).astype(jnp.float32)
    qc, kc, vc, gc = chunks(q), chunks(k), chunks(v), chunks(log_a)
    b = jnp.cumsum(gc, axis=3)
    b_last = b[:, :, :, -1:, :]
    q_dec = qc * jnp.exp(b)
    k_inv = kc * jnp.exp(-b)
    k_end = kc * jnp.exp(b_last - b)
    tri = jnp.tril(jnp.ones((GLA_CHUNK, GLA_CHUNK), dtype=bool))
    att = jnp.where(tri, jnp.einsum('bhntd,bhnsd->bhnts', q_dec, k_inv), 0.0)
    intra = jnp.einsum('bhnts,bhnse->bhnte', att, vc)
    u = jnp.einsum('bhnsd,bhnse->bhnde', k_end, vc)
    decay = jnp.exp(b_last[:, :, :, 0, :])
    def step(s, xs):
        dec, ui = xs
        return dec[..., None] * s + ui, s
    s_final, s_prev = lax.scan(step, s0, (jnp.moveaxis(decay, 2, 0), jnp.moveaxis(u, 2, 0)))
    s_prev = jnp.moveaxis(s_prev, 0, 2)
    inter = jnp.einsum('bhntd,bhnde->bhnte', q_dec, s_prev)
    o = (intra + inter).transpose(0, 2, 3, 1, 4).reshape(B, L, H, dv)
    return o, s_final


def gla_bidir(q, k, v, la_f, la_b, s0_f, s0_b):
    o_f, s_f = gla_scan(q, k, v, la_f, s0_f)
    flip = lambda t: jnp.flip(t, axis=1)
    o_b, s_b = gla_scan(flip(q), flip(k), flip(v), flip(la_b), s0_b)
    return o_f + flip(o_b), s_f, s_b


def gla_output(o, g, gain):
    B, L = o.shape[:2]
    return rmsnorm(o, gain).reshape(B, L, D_GLA) * jax.nn.silu(g)


def context_attention(q, k, v):
    B, L, H, dh = q.shape
    s = jnp.einsum('bqhd,bkhd->bhqk', q, k).astype(jnp.float32) * dh ** -0.5
    p = jax.nn.softmax(s, axis=-1).astype(v.dtype)
    return jnp.einsum('bhqk,bkhd->bqhd', p, v).reshape(B, L, H * dh)


def neighbourhood_attention(q, k, v, k_ctx, v_ctx, rpb, rows):
    B, S, H, dh = q.shape
    kr = min(NAT_KR, rows)
    scale = dh ** -0.5
    grid = lambda t: t.reshape(B, rows, GRID_W, H, dh).transpose(0, 3, 1, 2, 4)
    qg, kg, vg = grid(q), grid(k), grid(v)
    kcx, vcx = k_ctx.transpose(0, 2, 1, 3), v_ctx.transpose(0, 2, 1, 3)
    qcol = jnp.arange(GRID_W)
    cstart = jnp.clip(qcol - NAT_KC // 2, 0, GRID_W - NAT_KC)
    in_win = (qcol[None, :] >= cstart[:, None]) & (qcol[None, :] < cstart[:, None] + NAT_KC)
    col_idx = jnp.clip(qcol[None, :] - qcol[:, None] + NAT_KC - 1, 0, 2 * NAT_KC - 2)
    mask = jnp.broadcast_to(in_win[:, None, :], (GRID_W, kr, GRID_W)).reshape(GRID_W, kr * GRID_W)
    n_loc = kr * GRID_W
    def row_block(r):
        rs = jnp.clip(r - NAT_KR // 2, 0, rows - kr)
        kb = lax.dynamic_slice_in_dim(kg, rs, kr, axis=2).reshape(B, H, n_loc, dh)
        vb = lax.dynamic_slice_in_dim(vg, rs, kr, axis=2).reshape(B, H, n_loc, dh)
        qr = lax.dynamic_index_in_dim(qg, r, axis=2, keepdims=False)
        row_idx = rs + jnp.arange(kr) - r + NAT_KR - 1
        bias = rpb[:, row_idx][:, :, col_idx]
        bias = bias.transpose(0, 2, 1, 3).reshape(H, GRID_W, n_loc).astype(jnp.float32)
        s_loc = jnp.einsum('bhqd,bhkd->bhqk', qr, kb).astype(jnp.float32) * scale + bias
        s_loc = jnp.where(mask, s_loc, NEG)
        s_ctx = jnp.einsum('bhqd,bhkd->bhqk', qr, kcx).astype(jnp.float32) * scale
        p = jax.nn.softmax(jnp.concatenate([s_loc, s_ctx], axis=-1), axis=-1).astype(v.dtype)
        return (jnp.einsum('bhqk,bhkd->bhqd', p[..., :n_loc], vb)
                + jnp.einsum('bhqk,bhkd->bhqd', p[..., n_loc:], vcx))
    out = lax.map(row_block, jnp.arange(rows))
    return out.transpose(1, 0, 3, 2, 4).reshape(B, S, H * dh)


def dwconv3(x, w):
    xp = jnp.pad(x, ((0, 0), (1, 1), (0, 0)))
    return xp[:, :-2] * w[0] + xp[:, 1:-1] * w[1] + xp[:, 2:] * w[2]


def conv_ffn(h, w_up, w_conv, w_down):
    u = dwconv3(h @ w_up, w_conv)
    gate, val = jnp.split(u, 2, axis=-1)
    return (jax.nn.silu(gate) * val) @ w_down


def setup_inputs(seed: int = 0) -> dict:
    key = jax.random.key(seed)
    ks = jax.random.split(key, 20)
    nrm = lambda k, shape, s: jax.random.normal(k, shape, jnp.float32) * s
    return {
        'x': nrm(ks[0], (BATCH, SEQ, D_MODEL), 1.0),
        'c': nrm(ks[1], (BATCH, D_MODEL), 1.0),
        'ctx': nrm(ks[2], (BATCH, CTX_LEN, D_MODEL), 1.0),
        'c_ctx': nrm(ks[3], (D_MODEL,), 1.0),
        'ada_w': nrm(ks[4], (DEPTH, D_MODEL, N_MOD * D_MODEL), 0.5 * D_MODEL ** -0.5),
        'ada_b': nrm(ks[5], (DEPTH, N_MOD * D_MODEL), 0.02),
        'norm_mix_pre': 1.0 + nrm(ks[6], (DEPTH, D_MODEL), 0.02),
        'norm_mix_post': 1.0 + nrm(ks[7], (DEPTH, D_MODEL), 0.02),
        'w_in': nrm(ks[8], (DEPTH, D_MODEL, D_IN), D_MODEL ** -0.5),
        'gla_w_decay': nrm(ks[9], (DEPTH, 2, GLA_RANK, D_GLA_K), GLA_RANK ** -0.5),
        'gla_b_decay': nrm(ks[10], (DEPTH, 2, D_GLA_K), 0.1),
        'gla_norm': 1.0 + nrm(ks[11], (DEPTH, HEAD_DIM), 0.02),
        'fourier_w': nrm(ks[12], (DEPTH, N_FOURIER, HEAD_DIM, HEAD_DIM), HEAD_DIM ** -0.5),
        'nat_rpb': nrm(ks[13], (DEPTH, N_NAT, 2 * NAT_KR - 1, 2 * NAT_KC - 1), 0.02),
        'w_out': nrm(ks[14], (DEPTH, D_MIX, D_MODEL), D_MIX ** -0.5),
        'norm_ffn_pre': 1.0 + nrm(ks[15], (DEPTH, D_MODEL), 0.02),
        'norm_ffn_post': 1.0 + nrm(ks[16], (DEPTH, D_MODEL), 0.02),
        'ffn_w_up': nrm(ks[17], (DEPTH, D_MODEL, 2 * D_FF), D_MODEL ** -0.5),
        'ffn_w_conv': nrm(ks[18], (DEPTH, 3, 2 * D_FF), 3 ** -0.5),
        'ffn_w_down': nrm(ks[19], (DEPTH, D_FF, D_MODEL), D_FF ** -0.5),
    }


def reference(x, c, ctx, c_ctx, ada_w, ada_b, norm_mix_pre, norm_mix_post, w_in, gla_w_decay, gla_b_decay,
              gla_norm, fourier_w, nat_rpb, w_out, norm_ffn_pre, norm_ffn_post, ffn_w_up, ffn_w_conv, ffn_w_down):
    B, S, _ = x.shape
    Lc = ctx.shape[1]
    rows = S // GRID_W
    pos = jnp.arange(S)
    prow, pcol = pos // GRID_W, pos % GRID_W
    nat_heads = lambda t: t.reshape(t.shape[0], t.shape[1], N_NAT, HEAD_DIM)
    for l in range(DEPTH):
        last = l == DEPTH - 1
        mod_x = (jax.nn.silu(c) @ ada_w[l] + ada_b[l])[:, None, :]
        mod_c = (jax.nn.silu(c_ctx) @ ada_w[l] + ada_b[l])[None, None, :]
        sh1, sc1, g1, sh2, sc2, g2 = jnp.split(mod_x, N_MOD, axis=-1)
        csh1, csc1, cg1, csh2, csc2, cg2 = jnp.split(mod_c, N_MOD, axis=-1)

        px = modulate(x, norm_mix_pre[l], sh1, sc1) @ w_in[l]
        pc = modulate(ctx, norm_mix_pre[l], csh1, csc1) @ w_in[l]
        fx, qx, kx, vx, gx, afx, abx, nqx, nkx, nvx = split_proj(px)
        fc, qc, kc, vc, gc, afc, abc, nqc, nkc, nvc = split_proj(pc)

        four_x = fourier_mix(fx, fourier_w[l])

        qx_, kx_, vx_, lfx, lbx = gla_inputs(qx, kx, vx, afx, abx, gla_w_decay[l], gla_b_decay[l])
        qx_ = axial_rope(qx_, prow, pcol) * GLA_DK ** -0.5
        kx_ = axial_rope(kx_, prow, pcol)
        qc_, kc_, vc_, lfc, lbc = gla_inputs(qc, kc, vc, afc, abc, gla_w_decay[l], gla_b_decay[l])
        qc_ = qc_ * GLA_DK ** -0.5
        zero = jnp.zeros((B, N_GLA, GLA_DK, HEAD_DIM), jnp.float32)
        oc, sc_f, sc_b = gla_bidir(qc_, kc_, vc_, lfc, lbc, zero, zero)
        ox, _, _ = gla_bidir(qx_, kx_, vx_, lfx, lbx, sc_f, sc_b)
        gla_x = gla_output(ox.astype(x.dtype), gx, gla_norm[l])

        nat_x = neighbourhood_attention(nat_heads(nqx), nat_heads(nkx), nat_heads(nvx),
                                        nat_heads(nkc), nat_heads(nvc), nat_rpb[l], rows)

        mix_x = jnp.concatenate([four_x, gla_x, nat_x], axis=-1) @ w_out[l]
        x = x + g1 * rmsnorm(mix_x, norm_mix_post[l])
        if not last:
            four_c = fourier_mix(fc, fourier_w[l])
            gla_c = gla_output(oc.astype(ctx.dtype), gc, gla_norm[l])
            nat_c = context_attention(nat_heads(nqc), nat_heads(nkc), nat_heads(nvc))
            mix_c = jnp.concatenate([four_c, gla_c, nat_c], axis=-1) @ w_out[l]
            ctx = ctx + cg1 * rmsnorm(mix_c, norm_mix_post[l])

        fx_out = conv_ffn(modulate(x, norm_ffn_pre[l], sh2, sc2), ffn_w_up[l], ffn_w_conv[l], ffn_w_down[l])
        x = x + g2 * rmsnorm(fx_out, norm_ffn_post[l])
        if not last:
            fc_out = conv_ffn(modulate(ctx, norm_ffn_pre[l], csh2, csc2), ffn_w_up[l], ffn_w_conv[l], ffn_w_down[l])
            ctx = ctx + cg2 * rmsnorm(fc_out, norm_ffn_post[l])
    return x
```

```python
import functools

import numpy as np
import jax
import jax.numpy as jnp
from jax import lax
from jax.experimental import pallas as pl
from jax.experimental.pallas import tpu as pltpu

F32 = jnp.float32
BF16 = jnp.bfloat16

D_MODEL = 4096
GRID_W = 64
HEAD_DIM = 128
N_FOURIER = 8
N_GLA = 12
N_NAT = 12
D_FOURIER = N_FOURIER * HEAD_DIM
D_GLA = N_GLA * HEAD_DIM
GLA_DK = HEAD_DIM // 2
D_GLA_K = N_GLA * GLA_DK
GLA_RANK = 16
GLA_TAU = 16.0
GLA_CHUNK = 64
D_NAT = N_NAT * HEAD_DIM
NAT_KR = 8
NAT_KC = 16
ROPE_BASE = 10000.0
D_FF = 11 * D_MODEL // 8
N_MOD = 6
EPS = 1e-6
NEG = -1e30

LANES = 128
BF16_SUBLANES = 16
VMEM_LIMIT_MB = 56

PB_F, PB_V, PB_NQ, PB_NK, PB_NV = 0, 1024, 2560, 4096, 5632
N_PB = 7168
PF_Q, PF_K, PF_G, PF_A = 0, 768, 1536, 3072
N_PF = 4096
IN_BN = 1024


def _cparams(sem):
    return pltpu.CompilerParams(dimension_semantics=sem, vmem_limit_bytes=VMEM_LIMIT_MB * 1024 * 1024)


def _dot(a, b):
    return jnp.dot(a, b, preferred_element_type=F32)


def _dot_nt(a, b):
    return lax.dot_general(a, b, (((1,), (1,)), ((), ())), preferred_element_type=F32)


def _dot_tn(a, b):
    return lax.dot_general(a, b, (((0,), (0,)), ((), ())), preferred_element_type=F32)


def _split_bf16(x):
    hi = x.astype(BF16)
    lo = (x - hi.astype(F32)).astype(BF16)
    return hi, lo


def _ada_kernel(cc_ref, w_ref, b_ref, o_ref):
    s = cc_ref[...]
    s = s * jax.nn.sigmoid(s)
    s_hi = s.astype(BF16).astype(F32)
    row = lax.broadcasted_iota(jnp.int32, s.shape, 0)
    lhs = jnp.where(row < 8, s_hi, s - s_hi).astype(BF16)
    w_hi, w_lo = _split_bf16(w_ref[0])
    r = _dot(lhs, w_hi) + _dot(lhs, w_lo)
    o_ref[0] = r[0:8] + r[8:16] + b_ref[0]


def _ada_mod(c, c_ctx, ada_w, ada_b):
    depth, d, n = ada_w.shape
    bn = 512
    cc = jnp.zeros((16, d), F32)
    cc = cc.at[0].set(c[0]).at[1].set(c_ctx).at[8].set(c[0]).at[9].set(c_ctx)
    return pl.pallas_call(
        _ada_kernel,
        grid=(depth, n // bn),
        in_specs=[pl.BlockSpec((16, d), lambda l, j: (0, 0)),
                  pl.BlockSpec((1, d, bn), lambda l, j: (l, 0, j)),
                  pl.BlockSpec((1, 1, bn), lambda l, j: (l, 0, j))],
        out_specs=pl.BlockSpec((1, 8, bn), lambda l, j: (l, 0, j)),
        out_shape=jax.ShapeDtypeStruct((depth, 8, n), F32),
        compiler_params=_cparams(("arbitrary", "arbitrary")),
        name="ada_mod",
    )(cc, ada_w, ada_b.reshape(depth, 1, n))


def _modulate(x, gain, scale, shift):
    ms = jnp.mean(x * x, axis=-1, keepdims=True)
    y = x * lax.rsqrt(ms + EPS) * gain
    return y * (1.0 + scale) + shift


def _inproj_kernel(x_ref, gain_ref, sc_ref, sh_ref, w_ref, ob_ref, of_ref, h_ref, *, nb_tiles):
    j = pl.program_id(1)

    @pl.when(j == 0)
    def _():
        h_ref[...] = _modulate(x_ref[...], gain_ref[...], sc_ref[...], sh_ref[...]).astype(BF16)

    r = _dot(h_ref[...], w_ref[...])

    @pl.when(j < nb_tiles)
    def _():
        ob_ref[...] = r.astype(BF16)

    @pl.when(j >= nb_tiles)
    def _():
        of_ref[...] = r


def _inproj(x, gain, scale, shift, w):
    m, d = x.shape
    bm = min(512, m)
    bn = IN_BN
    nb_tiles = N_PB // bn
    nf_tiles = N_PF // bn
    vec = pl.BlockSpec((1, d), lambda i, j: (0, 0))
    return pl.pallas_call(
        functools.partial(_inproj_kernel, nb_tiles=nb_tiles),
        grid=(m // bm, nb_tiles + nf_tiles),
        in_specs=[pl.BlockSpec((bm, d), lambda i, j: (i, 0)), vec, vec, vec,
                  pl.BlockSpec((d, bn), lambda i, j: (0, j))],
        out_specs=[pl.BlockSpec((bm, bn), lambda i, j: (i, jnp.minimum(j, nb_tiles - 1))),
                   pl.BlockSpec((bm, bn), lambda i, j: (i, jnp.maximum(j - nb_tiles, 0)))],
        out_shape=[jax.ShapeDtypeStruct((m, N_PB), BF16), jax.ShapeDtypeStruct((m, N_PF), F32)],
        scratch_shapes=[pltpu.VMEM((bm, d), BF16)],
        compiler_params=_cparams(("arbitrary", "arbitrary")),
        name="inproj",
    )(x, gain, scale, shift, w)


def _four_w_kernel(cc_ref, sc_ref, fw_ref, o_ref):
    c_hi, c_lo = _split_bf16(cc_ref[...])
    s_hi, s_lo = _split_bf16(sc_ref[...])
    for g in range(N_FOURIER):
        w_hi, w_lo = _split_bf16(fw_ref[g])
        a = _dot(c_hi, w_hi) + _dot(c_hi, w_lo) + _dot(c_lo, w_hi)
        b = _dot(s_hi, w_hi) + _dot(s_hi, w_lo) + _dot(s_lo, w_hi)
        o_ref[g, :, 0:HEAD_DIM] = a
        o_ref[g, :, HEAD_DIM:2 * HEAD_DIM] = -b


def _four_weights(fourier_w_l):
    k = np.arange(HEAD_DIM)
    ang = 2.0 * np.pi * ((k[:, None] * k[None, :]) % HEAD_DIM) / HEAD_DIM
    cc = jnp.asarray(np.cos(ang), F32)
    sc = jnp.asarray(np.sin(ang), F32)
    return pl.pallas_call(
        _four_w_kernel,
        out_shape=jax.ShapeDtypeStruct((N_FOURIER, HEAD_DIM, 2 * HEAD_DIM), F32),
        name="four_w",
    )(cc, sc, fourier_w_l)


def _four_s0_kernel(u_ref, wc_ref, v_ref, *, norm):
    for g in range(N_FOURIER):
        cols = slice(g * HEAD_DIM, (g + 1) * HEAD_DIM)
        y = _dot(u_ref[:, cols], wc_ref[g].astype(BF16)) * norm
        v_ref[0, :, cols] = y[:, 0:HEAD_DIM].astype(BF16)
        v_ref[1, :, cols] = y[:, HEAD_DIM:2 * HEAD_DIM].astype(BF16)


def _four_s1_kernel(m1_ref, v_ref, z_ref):
    z_ref[...] = _dot(m1_ref[...].astype(BF16), v_ref[...]).astype(BF16)


def _four_s2_kernel(t_ref, z_ref, o_ref):
    n2 = z_ref.shape[2]
    z = z_ref[...].reshape(2 * n2, z_ref.shape[3])
    o_ref[...] = _dot(t_ref[0].astype(BF16), z).astype(BF16)


@functools.lru_cache(maxsize=None)
def _four_tables(n1, n2):
    length = n1 * n2
    m1 = None
    if n1 > 1:
        j = np.arange(n1)
        ang = 2.0 * np.pi * ((j[:, None] * j[None, :]) % n1) / n1
        cs, ss = np.cos(ang), np.sin(ang)
        m1 = np.block([[cs, ss], [-ss, cs]]).astype(np.float32)
    k1 = np.arange(n1, dtype=np.int64)[:, None, None]
    k2 = np.arange(n2, dtype=np.int64)[None, :, None]
    j2 = np.arange(n2, dtype=np.int64)[None, None, :]
    ang = 2.0 * np.pi * ((j2 * (k1 + n1 * k2)) % length) / length
    t2 = np.concatenate([np.cos(ang), np.sin(ang)], axis=-1).astype(np.float32)
    return m1, t2


def _fourier_mix(pb, wc):
    length = pb.shape[0]
    ch = D_FOURIER
    n2 = length if length <= 512 else 128
    n1 = length // n2
    m1, t2 = _four_tables(n1, n2)
    norm = float(1.0 / np.sqrt(float(length) * HEAD_DIM))
    bm = min(512, length)
    v = pl.pallas_call(
        functools.partial(_four_s0_kernel, norm=norm),
        grid=(length // bm,),
        in_specs=[pl.BlockSpec((bm, ch), lambda i: (i, 0)),
                  pl.BlockSpec((N_FOURIER, HEAD_DIM, 2 * HEAD_DIM), lambda i: (0, 0, 0))],
        out_specs=pl.BlockSpec((2, bm, ch), lambda i: (0, i, 0)),
        out_shape=jax.ShapeDtypeStruct((2, length, ch), BF16),
        compiler_params=_cparams(("arbitrary",)),
        name="four_s0",
    )(pb, wc)
    if n1 > 1:
        ncol = n2 * ch
        bc = min(4096, ncol)
        z = pl.pallas_call(
            _four_s1_kernel,
            grid=(ncol // bc,),
            in_specs=[pl.BlockSpec((2 * n1, 2 * n1), lambda i: (0, 0)),
                      pl.BlockSpec((2 * n1, bc), lambda i: (0, i))],
            out_specs=pl.BlockSpec((2 * n1, bc), lambda i: (0, i)),
            out_shape=jax.ShapeDtypeStruct((2 * n1, ncol), BF16),
            compiler_params=_cparams(("arbitrary",)),
            name="four_s1",
        )(jnp.asarray(m1), v.reshape(2 * n1, ncol))
    else:
        z = v
    z4 = z.reshape(2, n1, n2, ch)
    out = pl.pallas_call(
        _four_s2_kernel,
        grid=(n1,),
        in_specs=[pl.BlockSpec((1, n2, 2 * n2), lambda i: (i, 0, 0)),
                  pl.BlockSpec((2, 1, n2, ch), lambda i: (0, i, 0, 0))],
        out_specs=pl.BlockSpec((n2, ch), lambda i: (0, i)),
        out_shape=jax.ShapeDtypeStruct((n2, n1 * ch), BF16),
        compiler_params=_cparams(("arbitrary",)),
        name="four_s2",
    )(jnp.asarray(t2), z4)
    return out.reshape(length, ch)


def _rope_tables(length, use_rope):
    if not use_rope:
        return jnp.ones((length, LANES), F32), jnp.zeros((length, LANES), F32)
    pos = jnp.arange(length)
    prow = (pos // GRID_W).astype(F32)
    pcol = (pos % GRID_W).astype(F32)
    half = GLA_DK // 4
    freqs = ROPE_BASE ** (-jnp.arange(half, dtype=F32) / half)
    ar = prow[:, None] * freqs
    ac = pcol[:, None] * freqs
    cos64 = jnp.concatenate([jnp.cos(ar), jnp.cos(ar), jnp.cos(ac), jnp.cos(ac)], axis=-1)
    sin64 = jnp.concatenate([-jnp.sin(ar), jnp.sin(ar), -jnp.sin(ac), jnp.sin(ac)], axis=-1)
    return jnp.concatenate([cos64, cos64], axis=-1), jnp.concatenate([sin64, sin64], axis=-1)


def _rope(x, cos, sin):
    lane = lax.broadcasted_iota(jnp.int32, x.shape, 1)
    first = (lane % 32) < 16
    partner = jnp.where(first, pltpu.roll(x, LANES - 16, 1), pltpu.roll(x, 16, 1))
    return x * cos + partner * sin


def _gla_kernel(*refs, reverse, nchunk, nblk, finalize):
    if finalize:
        (q_ref, k_ref, a_ref, v_ref, cos_ref, sin_ref, wd_ref, bd_ref, tri_ref, s0_ref,
         of_ref, g_ref, gn_ref, o_ref, sfin_ref, st_ref) = refs
    else:
        (q_ref, k_ref, a_ref, v_ref, cos_ref, sin_ref, wd_ref, bd_ref, tri_ref, s0_ref,
         o_ref, sfin_ref, st_ref) = refs
    jb = pl.program_id(1)
    c = GLA_CHUNK

    @pl.when(jb == 0)
    def _():
        st_ref[...] = s0_ref[0]

    cos = cos_ref[...]
    sin = sin_ref[...]
    z = _dot(a_ref[...].astype(BF16), wd_ref[...].astype(BF16)) + bd_ref[...]
    g = (jnp.minimum(z, 0.0) - jnp.log1p(jnp.exp(-jnp.abs(z)))) * (1.0 / GLA_TAU)
    g_hi, g_lo = _split_bf16(g)
    bsum = _dot(tri_ref[...], jnp.concatenate([g_hi, g_lo], axis=1))
    b = bsum[:, 0:LANES] + bsum[:, LANES:2 * LANES]
    q_dec = _rope(q_ref[...], cos, sin) * (GLA_DK ** -0.5) * jnp.exp(b)
    k_inv = _rope(k_ref[...], cos, sin) * jnp.exp(-b)

    lane = lax.broadcasted_iota(jnp.int32, (c, LANES), 1)
    row = lax.broadcasted_iota(jnp.int32, (c, LANES), 0)
    head0 = lane < GLA_DK
    key_pos = lane % c
    causal = (key_pos >= row) if reverse else (key_pos <= row)
    lane2 = lax.broadcasted_iota(jnp.int32, (c, 2 * LANES), 1)
    vhead0 = lane2 < LANES
    srow = lax.broadcasted_iota(jnp.int32, (2 * HEAD_DIM, LANES), 0)
    slane = lax.broadcasted_iota(jnp.int32, (2 * HEAD_DIM, LANES), 1)
    state_mask = (srow < HEAD_DIM) == (slane < GLA_DK)

    s = st_ref[...]
    order = range(nchunk - 1, -1, -1) if reverse else range(nchunk)
    for ci in order:
        lo = ci * c
        last = lo if reverse else lo + c - 1
        dec = jnp.exp(b[last:last + 1, :])
        qd = q_dec[lo:lo + c, :].astype(BF16)
        ki = k_inv[lo:lo + c, :]
        ke = (ki * dec).astype(BF16)
        zero = jnp.zeros_like(ki)
        kbd = jnp.concatenate([jnp.where(head0, ki, zero), jnp.where(head0, zero, ki)], axis=0).astype(BF16)
        att = _dot_nt(qd, kbd)
        att = jnp.where(causal, att, 0.0).astype(BF16)
        vc = v_ref[lo:lo + c, :]
        vzero = jnp.zeros_like(vc)
        vbd = jnp.concatenate([jnp.where(vhead0, vc, vzero), jnp.where(vhead0, vzero, vc)], axis=0)
        o = _dot(att, vbd) + _dot_nt(qd, s.astype(BF16))
        if finalize:
            o = o + of_ref[lo:lo + c, :]
            gate = g_ref[lo:lo + c, :]
            gate = gate * jax.nn.sigmoid(gate)
            gn = gn_ref[...]
            outs = []
            for h in range(2):
                oh = o[:, h * HEAD_DIM:(h + 1) * HEAD_DIM]
                ms = jnp.mean(oh * oh, axis=-1, keepdims=True)
                outs.append(oh * lax.rsqrt(ms + EPS) * gn)
            o_ref[lo:lo + c, :] = (jnp.concatenate(outs, axis=1) * gate).astype(o_ref.dtype)
        else:
            o_ref[lo:lo + c, :] = o
        ut = _dot_tn(vc, ke)
        s = dec * s + jnp.where(state_mask, ut, 0.0)
    st_ref[...] = s

    @pl.when(jb == nblk - 1)
    def _():
        sfin_ref[0] = s


def _gla_tri(tb, reverse):
    t = np.arange(tb)
    same = (t[:, None] // GLA_CHUNK) == (t[None, :] // GLA_CHUNK)
    tri = (t[None, :] >= t[:, None]) if reverse else (t[None, :] <= t[:, None])
    return jnp.asarray((same & tri).astype(np.float32), BF16)


def _gla_pass(pf, pb, cos, sin, wd, bd, s0, reverse, fin=None):
    m = pf.shape[0]
    tb = min(512, m)
    nblk = m // tb
    npair = N_GLA // 2
    finalize = fin is not None

    def rowblk(j):
        return (nblk - 1 - j) if reverse else j

    in_specs = [
        pl.BlockSpec((tb, LANES), lambda h, j: (rowblk(j), PF_Q // LANES + h)),
        pl.BlockSpec((tb, LANES), lambda h, j: (rowblk(j), PF_K // LANES + h)),
        pl.BlockSpec((tb, LANES), lambda h, j: (rowblk(j), PF_A // LANES)),
        pl.BlockSpec((tb, 2 * LANES), lambda h, j: (rowblk(j), PB_V // (2 * LANES) + h)),
        pl.BlockSpec((tb, LANES), lambda h, j: (rowblk(j), 0)),
        pl.BlockSpec((tb, LANES), lambda h, j: (rowblk(j), 0)),
        pl.BlockSpec((LANES, LANES), lambda h, j: (0, h)),
        pl.BlockSpec((1, LANES), lambda h, j: (0, h)),
        pl.BlockSpec((tb, tb), lambda h, j: (0, 0)),
        pl.BlockSpec((1, 2 * HEAD_DIM, LANES), lambda h, j: (h, 0, 0)),
    ]
    args = [pf, pf, pf, pb, cos, sin, wd, bd, _gla_tri(tb, reverse), s0]
    if finalize:
        o_other, gn = fin
        in_specs += [
            pl.BlockSpec((tb, 2 * LANES), lambda h, j: (rowblk(j), h)),
            pl.BlockSpec((tb, 2 * LANES), lambda h, j: (rowblk(j), PF_G // (2 * LANES) + h)),
            pl.BlockSpec((1, HEAD_DIM), lambda h, j: (0, 0)),
        ]
        args += [o_other, pf, gn]
    out_dtype = BF16 if finalize else F32
    return pl.pallas_call(
        functools.partial(_gla_kernel, reverse=reverse, nchunk=tb // GLA_CHUNK, nblk=nblk, finalize=finalize),
        grid=(npair, nblk),
        in_specs=in_specs,
        out_specs=[pl.BlockSpec((tb, 2 * LANES), lambda h, j: (rowblk(j), h)),
                   pl.BlockSpec((1, 2 * HEAD_DIM, LANES), lambda h, j: (h, 0, 0))],
        out_shape=[jax.ShapeDtypeStruct((m, D_GLA), out_dtype),
                   jax.ShapeDtypeStruct((npair, 2 * HEAD_DIM, LANES), F32)],
        scratch_shapes=[pltpu.VMEM((2 * HEAD_DIM, LANES), F32)],
        compiler_params=_cparams(("arbitrary", "arbitrary")),
        name="gla_bwd" if reverse else "gla_fwd",
    )(*args)


def _gla_bidir(pf, pb, cos, sin, wd2, bd2, gn, s0_f, s0_b):
    o_f, s_f = _gla_pass(pf, pb, cos, sin, wd2[0], bd2[0], s0_f, reverse=False)
    out, s_b = _gla_pass(pf, pb, cos, sin, wd2[1], bd2[1], s0_b, reverse=True, fin=(o_f, gn))
    return out, s_f, s_b


def _nat_bias_table(rpb, ):
    qcol = np.arange(GRID_W)
    cstart = np.clip(qcol - NAT_KC // 2, 0, GRID_W - NAT_KC)
    in_win = (qcol[None, :] >= cstart[:, None]) & (qcol[None, :] < cstart[:, None] + NAT_KC)
    col_idx = np.clip(qcol[None, :] - qcol[:, None] + NAT_KC - 1, 0, 2 * NAT_KC - 2)
    d = np.arange(NAT_KR)[:, None]
    i = np.arange(NAT_KR)[None, :]
    row_idx = i - d + NAT_KR - 1
    bias = rpb[:, row_idx][:, :, :, col_idx]
    bias = jnp.where(jnp.asarray(in_win)[None, None, None], bias.astype(F32), NEG)
    bias = bias.transpose(1, 0, 3, 2, 4)
    return bias.reshape(NAT_KR, rpb.shape[0], GRID_W, NAT_KR * GRID_W)


def _nat_kernel(q_ref, k_ref, v_ref, kc_ref, vc_ref, mb_ref, o_ref, *, rb, rows):
    jb = pl.program_id(1)
    scale = HEAD_DIM ** -0.5
    kc = kc_ref[...]
    vc = vc_ref[...]
    nloc = NAT_KR * GRID_W
    for t in range(rb):
        r = jb * rb + t
        rs = jnp.clip(r - NAT_KR // 2, 0, rows - NAT_KR)
        d = r - rs
        start = pl.multiple_of(rs * GRID_W, GRID_W)
        q = q_ref[t * GRID_W:(t + 1) * GRID_W, :]
        kw = k_ref[pl.ds(start, nloc), :]
        vw = v_ref[pl.ds(start, nloc), :]
        s_loc = _dot_nt(q, kw) * scale + mb_ref[d, 0]
        s_ctx = _dot_nt(q, kc) * scale
        mx = jnp.maximum(jnp.max(s_loc, axis=-1, keepdims=True), jnp.max(s_ctx, axis=-1, keepdims=True))
        e_loc = jnp.exp(s_loc - mx)
        e_ctx = jnp.exp(s_ctx - mx)
        den = jnp.sum(e_loc, axis=-1, keepdims=True) + jnp.sum(e_ctx, axis=-1, keepdims=True)
        o = _dot(e_loc.astype(BF16), vw) + _dot(e_ctx.astype(BF16), vc)
        o_ref[t * GRID_W:(t + 1) * GRID_W, :] = (o / den).astype(BF16)


def _nat(pb, pbc, mb):
    length = pb.shape[0]
    lc = pbc.shape[0]
    rows = length // GRID_W
    rb = min(8, rows)
    return pl.pallas_call(
        functools.partial(_nat_kernel, rb=rb, rows=rows),
        grid=(N_NAT, rows // rb),
        in_specs=[pl.BlockSpec((rb * GRID_W, LANES), lambda h, j: (j, PB_NQ // LANES + h)),
                  pl.BlockSpec((length, LANES), lambda h, j: (0, PB_NK // LANES + h)),
                  pl.BlockSpec((length, LANES), lambda h, j: (0, PB_NV // LANES + h)),
                  pl.BlockSpec((lc, LANES), lambda h, j: (0, PB_NK // LANES + h)),
                  pl.BlockSpec((lc, LANES), lambda h, j: (0, PB_NV // LANES + h)),
                  pl.BlockSpec((NAT_KR, 1, GRID_W, NAT_KR * GRID_W), lambda h, j: (0, h, 0, 0))],
        out_specs=pl.BlockSpec((rb * GRID_W, LANES), lambda h, j: (j, h)),
        out_shape=jax.ShapeDtypeStruct((length, D_NAT), BF16),
        compiler_params=_cparams(("arbitrary", "arbitrary")),
        name="nat",
    )(pb, pb, pb, pbc, pbc, mb)


def _ctx_attn_kernel(q_ref, k_ref, v_ref, o_ref):
    s = _dot_nt(q_ref[...], k_ref[...]) * (HEAD_DIM ** -0.5)
    e = jnp.exp(s - jnp.max(s, axis=-1, keepdims=True))
    den = jnp.sum(e, axis=-1, keepdims=True)
    o_ref[...] = (_dot(e.astype(BF16), v_ref[...]) / den).astype(BF16)


def _ctx_attn(pbc):
    lc = pbc.shape[0]
    return pl.pallas_call(
        _ctx_attn_kernel,
        grid=(N_NAT,),
        in_specs=[pl.BlockSpec((lc, LANES), lambda h: (0, PB_NQ // LANES + h)),
                  pl.BlockSpec((lc, LANES), lambda h: (0, PB_NK // LANES + h)),
                  pl.BlockSpec((lc, LANES), lambda h: (0, PB_NV // LANES + h))],
        out_specs=pl.BlockSpec((lc, LANES), lambda h: (0, h)),
        out_shape=jax.ShapeDtypeStruct((lc, D_NAT), BF16),
        compiler_params=_cparams(("arbitrary",)),
        name="ctx_attn",
    )(pbc, pbc, pbc)


def _mm_kernel(*refs, splits):
    a_refs = refs[:len(splits)]
    w_ref = refs[len(splits)]
    o_ref = refs[len(splits) + 1]
    acc = None
    off = 0
    for a_ref, kk in zip(a_refs, splits):
        part = _dot(a_ref[...], w_ref[off:off + kk, :])
        acc = part if acc is None else acc + part
        off += kk
    o_ref[...] = acc


def _mm(a_list, w):
    m = a_list[0].shape[0]
    splits = tuple(a.shape[1] for a in a_list)
    k, n = w.shape
    bm = min(512, m)
    bn = 512
    in_specs = [pl.BlockSpec((bm, kk), lambda i, j: (i, 0)) for kk in splits]
    in_specs.append(pl.BlockSpec((k, bn), lambda i, j: (0, j)))
    return pl.pallas_call(
        functools.partial(_mm_kernel, splits=splits),
        grid=(m // bm, n // bn),
        in_specs=in_specs,
        out_specs=pl.BlockSpec((bm, bn), lambda i, j: (i, j)),
        out_shape=jax.ShapeDtypeStruct((m, n), F32),
        compiler_params=_cparams(("arbitrary", "arbitrary")),
        name="mm",
    )(*a_list, w)


def _post_kernel(x_ref, y_ref, gain_ref, gate_ref, o_ref):
    y = y_ref[...]
    ms = jnp.mean(y * y, axis=-1, keepdims=True)
    o_ref[...] = x_ref[...] + gate_ref[...] * (y * lax.rsqrt(ms + EPS) * gain_ref[...])


def _post(x, y, gain, gate):
    m, d = x.shape
    bm = min(256, m)
    blk = pl.BlockSpec((bm, d), lambda i: (i, 0))
    vec = pl.BlockSpec((1, d), lambda i: (0, 0))
    return pl.pallas_call(
        _post_kernel,
        grid=(m // bm,),
        in_specs=[blk, blk, vec, vec],
        out_specs=blk,
        out_shape=jax.ShapeDtypeStruct((m, d), F32),
        compiler_params=_cparams(("arbitrary",)),
        name="post",
    )(x, y, gain, gate)


FFN_HALO = BF16_SUBLANES
FFN_BN = 1024


def _ffn_up_kernel(x_ref, xp_ref, xn_ref, gain_ref, sc_ref, sh_ref, w_ref, cw_ref, o_ref, h_ref, *, bm, ni):
    i = pl.program_id(0)
    j = pl.program_id(1)
    hl = FFN_HALO

    @pl.when(j == 0)
    def _():
        gain, sc, sh = gain_ref[...], sc_ref[...], sh_ref[...]
        h_ref[hl:hl + bm, :] = _modulate(x_ref[...], gain, sc, sh).astype(BF16)
        hp = _modulate(xp_ref[...], gain, sc, sh)
        hn = _modulate(xn_ref[...], gain, sc, sh)
        h_ref[0:hl, :] = jnp.where(i > 0, hp, 0.0).astype(BF16)
        h_ref[hl + bm:2 * hl + bm, :] = jnp.where(i < ni - 1, hn, 0.0).astype(BF16)

    y = _dot(h_ref[...], w_ref[...])
    tot = bm + 2 * hl
    y_prev = pltpu.roll(y, 1, 0)[hl:hl + bm, :]
    y_next = pltpu.roll(y, tot - 1, 0)[hl:hl + bm, :]
    u = y_prev * cw_ref[0:1, :] + y[hl:hl + bm, :] * cw_ref[1:2, :] + y_next * cw_ref[2:3, :]
    half = u.shape[1] // 2
    gate = u[:, 0:half]
    o_ref[...] = (gate * jax.nn.sigmoid(gate) * u[:, half:]).astype(BF16)


def _ffn_up(x, gain, scale, shift, w, cw):
    m, d = x.shape
    bm = min(512, m)
    ni = m // bm
    bn = FFN_BN
    hl = FFN_HALO
    per = bm // hl
    nhalo = m // hl
    vec = pl.BlockSpec((1, d), lambda i, j: (0, 0))
    return pl.pallas_call(
        functools.partial(_ffn_up_kernel, bm=bm, ni=ni),
        grid=(ni, w.shape[1] // bn),
        in_specs=[pl.BlockSpec((bm, d), lambda i, j: (i, 0)),
                  pl.BlockSpec((hl, d), lambda i, j: (jnp.maximum(i * per - 1, 0), 0)),
                  pl.BlockSpec((hl, d), lambda i, j: (jnp.minimum((i + 1) * per, nhalo - 1), 0)),
                  vec, vec, vec,
                  pl.BlockSpec((d, bn), lambda i, j: (0, j)),
                  pl.BlockSpec((3, bn), lambda i, j: (0, j))],
        out_specs=pl.BlockSpec((bm, bn // 2), lambda i, j: (i, j)),
        out_shape=jax.ShapeDtypeStruct((m, w.shape[1] // 2), BF16),
        scratch_shapes=[pltpu.VMEM((bm + 2 * hl, d), BF16)],
        compiler_params=_cparams(("arbitrary", "arbitrary")),
        name="ffn_up",
    )(x, x, x, gain, scale, shift, w, cw)


def _prep_w_in(w):
    sizes = (D_FOURIER, D_GLA_K, D_GLA_K, D_GLA, D_GLA, GLA_RANK, GLA_RANK, D_NAT, D_NAT, D_NAT)
    offs = np.concatenate([[0], np.cumsum(sizes)])
    f, q, k, v, g, af, ab, nq, nk, nv = [w[:, int(offs[t]):int(offs[t + 1])] for t in range(10)]
    pad = jnp.zeros((w.shape[0], N_PF - PF_A - 2 * GLA_RANK), w.dtype)
    return jnp.concatenate([f, v, nq, nk, nv, q, k, g, af, ab, pad], axis=1).astype(BF16)


def _prep_w_up(w, cw):
    half = FFN_BN // 2
    nt = D_FF // half

    def mix(t):
        gate = t[:, :D_FF].reshape(t.shape[0], nt, half)
        val = t[:, D_FF:].reshape(t.shape[0], nt, half)
        return jnp.concatenate([gate, val], axis=2).reshape(t.shape[0], 2 * D_FF)

    return mix(w).astype(BF16), mix(cw)


def _prep_decay(w_dec, b_dec):
    wd = jnp.zeros((2, LANES, D_GLA_K), F32)
    wd = wd.at[0, 0:GLA_RANK].set(w_dec[0]).at[1, GLA_RANK:2 * GLA_RANK].set(w_dec[1])
    return wd, b_dec.reshape(2, 1, D_GLA_K)


def _state_zero():
    return jnp.zeros((N_GLA // 2, 2 * HEAD_DIM, LANES), F32)


def kernel(x, c, ctx, c_ctx, ada_w, ada_b, norm_mix_pre, norm_mix_post, w_in, gla_w_decay, gla_b_decay,
           gla_norm, fourier_w, nat_rpb, w_out, norm_ffn_pre, norm_ffn_post, ffn_w_up, ffn_w_conv, ffn_w_down):
    depth = ada_w.shape[0]
    d = x.shape[-1]
    xs = x[0]
    cs = ctx[0]
    seq = xs.shape[0]
    lc = cs.shape[0]

    mod = _ada_mod(c, c_ctx, ada_w, ada_b)
    cos_x, sin_x = _rope_tables(seq, True)
    cos_c, sin_c = _rope_tables(lc, False)

    for l in range(depth):
        last = l == depth - 1
        mx = [mod[l, 0:1, t * d:(t + 1) * d] for t in range(N_MOD)]
        mc = [mod[l, 1:2, t * d:(t + 1) * d] for t in range(N_MOD)]
        vrow = lambda t: t[l].reshape(1, -1)

        w_in_l = _prep_w_in(w_in[l])
        w_out_l = w_out[l].astype(BF16)
        w_up_l, cw_l = _prep_w_up(ffn_w_up[l], ffn_w_conv[l])
        w_down_l = ffn_w_down[l].astype(BF16)
        wd2, bd2 = _prep_decay(gla_w_decay[l], gla_b_decay[l])
        wc = _four_weights(fourier_w[l])
        mb = _nat_bias_table(nat_rpb[l])
        gn = vrow(gla_norm)

        pbx, pfx = _inproj(xs, vrow(norm_mix_pre), mx[1], mx[0], w_in_l)
        pbc, pfc = _inproj(cs, vrow(norm_mix_pre), mc[1], mc[0], w_in_l)

        gla_c, sc_f, sc_b = _gla_bidir(pfc, pbc, cos_c, sin_c, wd2, bd2, gn, _state_zero(), _state_zero())
        gla_x, _, _ = _gla_bidir(pfx, pbx, cos_x, sin_x, wd2, bd2, gn, sc_f, sc_b)
        four_x = _fourier_mix(pbx, wc)
        nat_x = _nat(pbx, pbc, mb)
        mix_x = _mm([four_x, gla_x, nat_x], w_out_l)
        xs = _post(xs, mix_x, vrow(norm_mix_post), mx[2])
        if not last:
            four_c = _fourier_mix(pbc, wc)
            nat_c = _ctx_attn(pbc)
            mix_c = _mm([four_c, gla_c, nat_c], w_out_l)
            cs = _post(cs, mix_c, vrow(norm_mix_post), mc[2])

        act_x = _ffn_up(xs, vrow(norm_ffn_pre), mx[4], mx[3], w_up_l, cw_l)
        xs = _post(xs, _mm([act_x], w_down_l), vrow(norm_ffn_post), mx[5])
        if not last:
            act_c = _ffn_up(cs, vrow(norm_ffn_pre), mc[4], mc[3], w_up_l, cw_l)
            cs = _post(cs, _mm([act_c], w_down_l), vrow(norm_ffn_post), mc[5])
    return xs[None]
```

```python
import functools

import numpy as np
import jax
import jax.numpy as jnp
from jax import lax
from jax.experimental import pallas as pl
from jax.experimental.pallas import tpu as pltpu

F32 = jnp.float32
BF16 = jnp.bfloat16

D_MODEL = 4096
GRID_W = 64
HEAD_DIM = 128
N_FOURIER = 8
N_GLA = 12
N_NAT = 12
D_FOURIER = N_FOURIER * HEAD_DIM
D_GLA = N_GLA * HEAD_DIM
GLA_DK = HEAD_DIM // 2
D_GLA_K = N_GLA * GLA_DK
GLA_RANK = 16
GLA_TAU = 16.0
GLA_CHUNK = 64
D_NAT = N_NAT * HEAD_DIM
NAT_KR = 8
NAT_KC = 16
ROPE_BASE = 10000.0
D_FF = 11 * D_MODEL // 8
N_MOD = 6
EPS = 1e-6
NEG = -1e30

LANES = 128
BF16_SUBLANES = 16
VMEM_LIMIT_MB = 56

PB_F, PB_V, PB_NQ, PB_NK, PB_NV = 0, 1024, 2560, 4096, 5632
N_PB = 7168
PF_Q, PF_K, PF_G = 0, 768, 1536
N_PF = 3072
IN_BN = 1024
GLA_SLABS = 2


def _cparams(sem):
    return pltpu.CompilerParams(dimension_semantics=sem, vmem_limit_bytes=VMEM_LIMIT_MB * 1024 * 1024)


def _dot(a, b):
    return jnp.dot(a, b, preferred_element_type=F32)


def _dot_nt(a, b):
    return lax.dot_general(a, b, (((1,), (1,)), ((), ())), preferred_element_type=F32)


def _dot_tn(a, b):
    return lax.dot_general(a, b, (((0,), (0,)), ((), ())), preferred_element_type=F32)


def _split_bf16(x):
    hi = x.astype(BF16)
    lo = (x - hi.astype(F32)).astype(BF16)
    return hi, lo


def _ada_kernel(cc_ref, w_ref, b_ref, o_ref):
    s = cc_ref[...]
    s = s * jax.nn.sigmoid(s)
    s_hi = s.astype(BF16).astype(F32)
    row = lax.broadcasted_iota(jnp.int32, s.shape, 0)
    lhs = jnp.where(row < 8, s_hi, s - s_hi).astype(BF16)
    w_hi, w_lo = _split_bf16(w_ref[0])
    r = _dot(lhs, w_hi) + _dot(lhs, w_lo)
    o_ref[0] = r[0:8] + r[8:16] + b_ref[0]


def _ada_mod(c, c_ctx, ada_w, ada_b):
    depth, d, n = ada_w.shape
    bn = 512
    cc = jnp.zeros((16, d), F32)
    cc = cc.at[0].set(c[0]).at[1].set(c_ctx).at[8].set(c[0]).at[9].set(c_ctx)
    return pl.pallas_call(
        _ada_kernel,
        grid=(depth, n // bn),
        in_specs=[pl.BlockSpec((16, d), lambda l, j: (0, 0)),
                  pl.BlockSpec((1, d, bn), lambda l, j: (l, 0, j)),
                  pl.BlockSpec((1, 1, bn), lambda l, j: (l, 0, j))],
        out_specs=pl.BlockSpec((1, 8, bn), lambda l, j: (l, 0, j)),
        out_shape=jax.ShapeDtypeStruct((depth, 8, n), F32),
        compiler_params=_cparams(("arbitrary", "arbitrary")),
        name="ada_mod",
    )(cc, ada_w, ada_b.reshape(depth, 1, n))


def _modulate(x, gain, scale, shift):
    ms = jnp.mean(x * x, axis=-1, keepdims=True)
    y = x * lax.rsqrt(ms + EPS) * gain
    return y * (1.0 + scale) + shift


def _inproj_kernel(x_ref, gain_ref, sc_ref, sh_ref, w_ref, wa_ref, ob_ref, of_ref, oa_ref, h_ref, *, nb_tiles):
    j = pl.program_id(1)

    @pl.when(j == 0)
    def _():
        h = _modulate(x_ref[...], gain_ref[...], sc_ref[...], sh_ref[...]).astype(BF16)
        h_ref[...] = h
        oa_ref[...] = _dot(h, wa_ref[0])

    r = _dot(h_ref[...], w_ref[0])

    @pl.when(j < nb_tiles)
    def _():
        ob_ref[...] = r.astype(BF16)

    @pl.when(j >= nb_tiles)
    def _():
        of_ref[...] = r


def _inproj(x, gain, scale, shift, w, wa, l):
    m, d = x.shape
    bm = min(512, m)
    bn = IN_BN
    nb_tiles = N_PB // bn
    nf_tiles = N_PF // bn
    vec = pl.BlockSpec((1, d), lambda i, j: (0, 0))
    return pl.pallas_call(
        functools.partial(_inproj_kernel, nb_tiles=nb_tiles),
        grid=(m // bm, nb_tiles + nf_tiles),
        in_specs=[pl.BlockSpec((bm, d), lambda i, j: (i, 0)), vec, vec, vec,
                  pl.BlockSpec((1, d, bn), lambda i, j: (l, 0, j)),
                  pl.BlockSpec((1, d, LANES), lambda i, j: (l, 0, 0))],
        out_specs=[pl.BlockSpec((bm, bn), lambda i, j: (i, jnp.minimum(j, nb_tiles - 1))),
                   pl.BlockSpec((bm, bn), lambda i, j: (i, jnp.maximum(j - nb_tiles, 0))),
                   pl.BlockSpec((bm, LANES), lambda i, j: (i, 0))],
        out_shape=[jax.ShapeDtypeStruct((m, N_PB), BF16), jax.ShapeDtypeStruct((m, N_PF), F32),
                   jax.ShapeDtypeStruct((m, LANES), F32)],
        scratch_shapes=[pltpu.VMEM((bm, d), BF16)],
        compiler_params=_cparams(("arbitrary", "arbitrary")),
        name="inproj",
    )(x, gain, scale, shift, w, wa)


def _four_w_kernel(cc_ref, sc_ref, fw_ref, o_ref):
    c_hi, c_lo = _split_bf16(cc_ref[...])
    s_hi, s_lo = _split_bf16(sc_ref[...])
    for g in range(N_FOURIER):
        w_hi, w_lo = _split_bf16(fw_ref[g])
        a = _dot(c_hi, w_hi) + _dot(c_hi, w_lo) + _dot(c_lo, w_hi)
        b = _dot(s_hi, w_hi) + _dot(s_hi, w_lo) + _dot(s_lo, w_hi)
        o_ref[g, :, 0:HEAD_DIM] = a
        o_ref[g, :, HEAD_DIM:2 * HEAD_DIM] = -b


def _four_weights(fourier_w_l):
    k = np.arange(HEAD_DIM)
    ang = 2.0 * np.pi * ((k[:, None] * k[None, :]) % HEAD_DIM) / HEAD_DIM
    cc = jnp.asarray(np.cos(ang), F32)
    sc = jnp.asarray(np.sin(ang), F32)
    return pl.pallas_call(
        _four_w_kernel,
        out_shape=jax.ShapeDtypeStruct((N_FOURIER, HEAD_DIM, 2 * HEAD_DIM), F32),
        name="four_w",
    )(cc, sc, fourier_w_l)


def _four_s0_kernel(u_ref, wc_ref, v_ref, *, norm):
    for g in range(N_FOURIER):
        cols = slice(g * HEAD_DIM, (g + 1) * HEAD_DIM)
        y = _dot(u_ref[:, cols], wc_ref[g].astype(BF16)) * norm
        v_ref[0, :, cols] = y[:, 0:HEAD_DIM].astype(BF16)
        v_ref[1, :, cols] = y[:, HEAD_DIM:2 * HEAD_DIM].astype(BF16)


def _four_s1_kernel(m1_ref, v_ref, z_ref):
    z_ref[...] = _dot(m1_ref[...].astype(BF16), v_ref[...]).astype(BF16)


def _four_s2_kernel(t_ref, z_ref, o_ref):
    n2 = z_ref.shape[2]
    z = z_ref[...].reshape(2 * n2, z_ref.shape[3])
    o_ref[...] = _dot(t_ref[0].astype(BF16), z).astype(BF16)


@functools.lru_cache(maxsize=None)
def _four_tables(n1, n2):
    length = n1 * n2
    m1 = None
    if n1 > 1:
        j = np.arange(n1)
        ang = 2.0 * np.pi * ((j[:, None] * j[None, :]) % n1) / n1
        cs, ss = np.cos(ang), np.sin(ang)
        m1 = np.block([[cs, ss], [-ss, cs]]).astype(np.float32)
    k1 = np.arange(n1, dtype=np.int64)[:, None, None]
    k2 = np.arange(n2, dtype=np.int64)[None, :, None]
    j2 = np.arange(n2, dtype=np.int64)[None, None, :]
    ang = 2.0 * np.pi * ((j2 * (k1 + n1 * k2)) % length) / length
    t2 = np.concatenate([np.cos(ang), np.sin(ang)], axis=-1).astype(np.float32)
    return m1, t2


def _fourier_mix(pb, wc):
    length = pb.shape[0]
    ch = D_FOURIER
    n2 = length if length <= 512 else 128
    n1 = length // n2
    m1, t2 = _four_tables(n1, n2)
    norm = float(1.0 / np.sqrt(float(length) * HEAD_DIM))
    bm = min(512, length)
    v = pl.pallas_call(
        functools.partial(_four_s0_kernel, norm=norm),
        grid=(length // bm,),
        in_specs=[pl.BlockSpec((bm, ch), lambda i: (i, 0)),
                  pl.BlockSpec((N_FOURIER, HEAD_DIM, 2 * HEAD_DIM), lambda i: (0, 0, 0))],
        out_specs=pl.BlockSpec((2, bm, ch), lambda i: (0, i, 0)),
        out_shape=jax.ShapeDtypeStruct((2, length, ch), BF16),
        compiler_params=_cparams(("arbitrary",)),
        name="four_s0",
    )(pb, wc)
    if n1 > 1:
        ncol = n2 * ch
        bc = min(4096, ncol)
        z = pl.pallas_call(
            _four_s1_kernel,
            grid=(ncol // bc,),
            in_specs=[pl.BlockSpec((2 * n1, 2 * n1), lambda i: (0, 0)),
                      pl.BlockSpec((2 * n1, bc), lambda i: (0, i))],
            out_specs=pl.BlockSpec((2 * n1, bc), lambda i: (0, i)),
            out_shape=jax.ShapeDtypeStruct((2 * n1, ncol), BF16),
            compiler_params=_cparams(("arbitrary",)),
            name="four_s1",
        )(jnp.asarray(m1), v.reshape(2 * n1, ncol))
    else:
        z = v
    z4 = z.reshape(2, n1, n2, ch)
    out = pl.pallas_call(
        _four_s2_kernel,
        grid=(n1,),
        in_specs=[pl.BlockSpec((1, n2, 2 * n2), lambda i: (i, 0, 0)),
                  pl.BlockSpec((2, 1, n2, ch), lambda i: (0, i, 0, 0))],
        out_specs=pl.BlockSpec((n2, ch), lambda i: (0, i)),
        out_shape=jax.ShapeDtypeStruct((n2, n1 * ch), BF16),
        compiler_params=_cparams(("arbitrary",)),
        name="four_s2",
    )(jnp.asarray(t2), z4)
    return out.reshape(length, ch)


def _rope_tables(length, use_rope):
    if not use_rope:
        return jnp.ones((length, LANES), F32), jnp.zeros((length, LANES), F32)
    pos = jnp.arange(length)
    prow = (pos // GRID_W).astype(F32)
    pcol = (pos % GRID_W).astype(F32)
    half = GLA_DK // 4
    freqs = ROPE_BASE ** (-jnp.arange(half, dtype=F32) / half)
    ar = prow[:, None] * freqs
    ac = pcol[:, None] * freqs
    cos64 = jnp.concatenate([jnp.cos(ar), jnp.cos(ar), jnp.cos(ac), jnp.cos(ac)], axis=-1)
    sin64 = jnp.concatenate([-jnp.sin(ar), jnp.sin(ar), -jnp.sin(ac), jnp.sin(ac)], axis=-1)
    return jnp.concatenate([cos64, cos64], axis=-1), jnp.concatenate([sin64, sin64], axis=-1)


def _rope(x, cos, sin):
    lane = lax.broadcasted_iota(jnp.int32, x.shape, 1)
    first = (lane % 32) < 16
    partner = jnp.where(first, pltpu.roll(x, LANES - 16, 1), pltpu.roll(x, 16, 1))
    return x * cos + partner * sin


def _gla_kernel(*refs, reverse, nchunk, nblk, finalize):
    if finalize:
        (q_ref, k_ref, a_ref, v_ref, cos_ref, sin_ref, wd_ref, bd_ref, tri_ref, s0_ref,
         of_ref, g_ref, gn_ref, o_ref, sfin_ref, st_ref) = refs
    else:
        (q_ref, k_ref, a_ref, v_ref, cos_ref, sin_ref, wd_ref, bd_ref, tri_ref, s0_ref,
         o_ref, sfin_ref, st_ref) = refs
        of_ref = g_ref = gn_ref = None
    jb = pl.program_id(1)

    @pl.when(jb == 0)
    def _():
        st_ref[...] = s0_ref[...]

    finals = []
    for sl in range(GLA_SLABS):
        one = slice(sl * LANES, (sl + 1) * LANES)
        two = slice(2 * sl * LANES, 2 * (sl + 1) * LANES)
        finals.append(_gla_slab(
            q_ref[:, one], k_ref[:, one], a_ref[...], v_ref, two, cos_ref[...], sin_ref[...],
            wd_ref[:, one], bd_ref[:, one], tri_ref[...], st_ref[sl],
            of_ref, g_ref, gn_ref, o_ref, reverse=reverse, nchunk=nchunk, finalize=finalize))
    for sl in range(GLA_SLABS):
        st_ref[sl] = finals[sl]

    @pl.when(jb == nblk - 1)
    def _():
        for sl in range(GLA_SLABS):
            sfin_ref[sl] = finals[sl]


def _gla_slab(q, k, a, v_ref, two, cos, sin, wd, bd, tri, s, of_ref, g_ref, gn_ref, o_ref, *,
              reverse, nchunk, finalize):
    c = GLA_CHUNK
    z = _dot(a.astype(BF16), wd.astype(BF16)) + bd
    g = (jnp.minimum(z, 0.0) - jnp.log1p(jnp.exp(-jnp.abs(z)))) * (1.0 / GLA_TAU)
    g_hi, g_lo = _split_bf16(g)
    bsum = _dot(tri, jnp.concatenate([g_hi, g_lo], axis=1))
    b = bsum[:, 0:LANES] + bsum[:, LANES:2 * LANES]
    q_dec = _rope(q, cos, sin) * (GLA_DK ** -0.5) * jnp.exp(b)
    k_inv = _rope(k, cos, sin) * jnp.exp(-b)

    lane = lax.broadcasted_iota(jnp.int32, (c, LANES), 1)
    row = lax.broadcasted_iota(jnp.int32, (c, LANES), 0)
    head0 = lane < GLA_DK
    key_pos = lane % c
    causal = (key_pos >= row) if reverse else (key_pos <= row)
    lane2 = lax.broadcasted_iota(jnp.int32, (c, 2 * LANES), 1)
    vhead0 = lane2 < LANES
    srow = lax.broadcasted_iota(jnp.int32, (2 * HEAD_DIM, LANES), 0)
    slane = lax.broadcasted_iota(jnp.int32, (2 * HEAD_DIM, LANES), 1)
    state_mask = (srow < HEAD_DIM) == (slane < GLA_DK)

    order = range(nchunk - 1, -1, -1) if reverse else range(nchunk)
    for ci in order:
        lo = ci * c
        last = lo if reverse else lo + c - 1
        dec = jnp.exp(b[last:last + 1, :])
        qd = q_dec[lo:lo + c, :].astype(BF16)
        ki = k_inv[lo:lo + c, :]
        ke = (ki * dec).astype(BF16)
        zero = jnp.zeros_like(ki)
        kbd = jnp.concatenate([jnp.where(head0, ki, zero), jnp.where(head0, zero, ki)], axis=0).astype(BF16)
        att = _dot_nt(qd, kbd)
        att = jnp.where(causal, att, 0.0).astype(BF16)
        vc = v_ref[lo:lo + c, two]
        vzero = jnp.zeros_like(vc)
        vbd = jnp.concatenate([jnp.where(vhead0, vc, vzero), jnp.where(vhead0, vzero, vc)], axis=0)
        o = _dot(att, vbd) + _dot_nt(qd, s.astype(BF16))
        if finalize:
            o = o + of_ref[lo:lo + c, two]
            gate = g_ref[lo:lo + c, two]
            gate = gate * jax.nn.sigmoid(gate)
            gn = gn_ref[...]
            outs = []
            for h in range(2):
                oh = o[:, h * HEAD_DIM:(h + 1) * HEAD_DIM]
                ms = jnp.mean(oh * oh, axis=-1, keepdims=True)
                outs.append(oh * lax.rsqrt(ms + EPS) * gn)
            o_ref[lo:lo + c, two] = (jnp.concatenate(outs, axis=1) * gate).astype(o_ref.dtype)
        else:
            o_ref[lo:lo + c, two] = o
        ut = _dot_tn(vc, ke)
        s = dec * s + jnp.where(state_mask, ut, 0.0)
    return s


def _gla_tri(tb, reverse):
    t = np.arange(tb)
    same = (t[:, None] // GLA_CHUNK) == (t[None, :] // GLA_CHUNK)
    tri = (t[None, :] >= t[:, None]) if reverse else (t[None, :] <= t[:, None])
    return jnp.asarray((same & tri).astype(np.float32), BF16)


def _gla_pass(pf, pa, pb, cos, sin, wd, bd, s0, reverse, fin=None):
    m = pf.shape[0]
    tb = min(512, m)
    nblk = m // tb
    npair = N_GLA // 2
    finalize = fin is not None

    def rowblk(j):
        return (nblk - 1 - j) if reverse else j

    ns = GLA_SLABS
    w1 = ns * LANES
    w2 = 2 * w1
    in_specs = [
        pl.BlockSpec((tb, w1), lambda h, j: (rowblk(j), PF_Q // w1 + h)),
        pl.BlockSpec((tb, w1), lambda h, j: (rowblk(j), PF_K // w1 + h)),
        pl.BlockSpec((tb, LANES), lambda h, j: (rowblk(j), 0)),
        pl.BlockSpec((tb, w2), lambda h, j: (rowblk(j), PB_V // w2 + h)),
        pl.BlockSpec((tb, LANES), lambda h, j: (rowblk(j), 0)),
        pl.BlockSpec((tb, LANES), lambda h, j: (rowblk(j), 0)),
        pl.BlockSpec((LANES, w1), lambda h, j: (0, h)),
        pl.BlockSpec((1, w1), lambda h, j: (0, h)),
        pl.BlockSpec((tb, tb), lambda h, j: (0, 0)),
        pl.BlockSpec((ns, 2 * HEAD_DIM, LANES), lambda h, j: (h, 0, 0)),
    ]
    args = [pf, pf, pa, pb, cos, sin, wd, bd, _gla_tri(tb, reverse), s0]
    if finalize:
        o_other, gn = fin
        in_specs += [
            pl.BlockSpec((tb, w2), lambda h, j: (rowblk(j), h)),
            pl.BlockSpec((tb, w2), lambda h, j: (rowblk(j), PF_G // w2 + h)),
            pl.BlockSpec((1, HEAD_DIM), lambda h, j: (0, 0)),
        ]
        args += [o_other, pf, gn]
    out_dtype = BF16 if finalize else F32
    return pl.pallas_call(
        functools.partial(_gla_kernel, reverse=reverse, nchunk=tb // GLA_CHUNK, nblk=nblk, finalize=finalize),
        grid=(npair // ns, nblk),
        in_specs=in_specs,
        out_specs=[pl.BlockSpec((tb, w2), lambda h, j: (rowblk(j), h)),
                   pl.BlockSpec((ns, 2 * HEAD_DIM, LANES), lambda h, j: (h, 0, 0))],
        out_shape=[jax.ShapeDtypeStruct((m, D_GLA), out_dtype),
                   jax.ShapeDtypeStruct((npair, 2 * HEAD_DIM, LANES), F32)],
        scratch_shapes=[pltpu.VMEM((ns, 2 * HEAD_DIM, LANES), F32)],
        compiler_params=_cparams(("arbitrary", "arbitrary")),
        name="gla_bwd" if reverse else "gla_fwd",
    )(*args)


def _gla_bidir(pf, pa, pb, cos, sin, wd2, bd2, gn, s0_f, s0_b):
    o_f, s_f = _gla_pass(pf, pa, pb, cos, sin, wd2[0], bd2[0], s0_f, reverse=False)
    out, s_b = _gla_pass(pf, pa, pb, cos, sin, wd2[1], bd2[1], s0_b, reverse=True, fin=(o_f, gn))
    return out, s_f, s_b


def _nat_bias_table(rpb_all):
    qcol = np.arange(GRID_W)
    cstart = np.clip(qcol - NAT_KC // 2, 0, GRID_W - NAT_KC)
    in_win = (qcol[None, :] >= cstart[:, None]) & (qcol[None, :] < cstart[:, None] + NAT_KC)
    col_idx = np.clip(qcol[None, :] - qcol[:, None] + NAT_KC - 1, 0, 2 * NAT_KC - 2)
    lane = np.arange(2 * GRID_W)
    npair = 2 * NAT_KR - 2
    rho = np.arange(npair)[:, None, None] + (lane // GRID_W)[None, None, :]
    col2 = col_idx[:, lane % GRID_W][None]
    win2 = in_win[:, lane % GRID_W]
    bias = rpb_all[:, :, rho, col2].astype(F32)
    return jnp.where(jnp.asarray(win2), bias, NEG)


def _nat_kernel(q_ref, k_ref, v_ref, kc_ref, vc_ref, tt_ref, o_ref, *, rb, rows):
    jb = pl.program_id(1)
    scale = HEAD_DIM ** -0.5
    nloc = NAT_KR * GRID_W
    q_all = q_ref[...]
    s_ctx = _dot_nt(q_all, kc_ref[...]) * scale
    starts = []
    s_rows = []
    for t in range(rb):
        r = jb * rb + t
        rs = jnp.clip(r - NAT_KR // 2, 0, rows - NAT_KR)
        d = r - rs
        start = pl.multiple_of(rs * GRID_W, GRID_W)
        starts.append(start)
        kw = k_ref[pl.ds(start, nloc), :]
        bias = jnp.concatenate([tt_ref[0, 0, NAT_KR - 1 - d + 2 * p] for p in range(NAT_KR // 2)], axis=1)
        s_rows.append(_dot_nt(q_all[t * GRID_W:(t + 1) * GRID_W, :], kw) * scale + bias)
    s_loc = jnp.concatenate(s_rows, axis=0)
    mx = jnp.maximum(jnp.max(s_loc, axis=-1, keepdims=True), jnp.max(s_ctx, axis=-1, keepdims=True))
    e_loc = jnp.exp(s_loc - mx)
    e_ctx = jnp.exp(s_ctx - mx)
    den = jnp.sum(e_loc, axis=-1, keepdims=True) + jnp.sum(e_ctx, axis=-1, keepdims=True)
    e_loc = e_loc.astype(BF16)
    o_ctx = _dot(e_ctx.astype(BF16), vc_ref[...])
    o_rows = [_dot(e_loc[t * GRID_W:(t + 1) * GRID_W, :], v_ref[pl.ds(starts[t], nloc), :]) for t in range(rb)]
    o_ref[...] = ((jnp.concatenate(o_rows, axis=0) + o_ctx) / den).astype(BF16)


def _nat(pb, pbc, tt, l):
    length = pb.shape[0]
    lc = pbc.shape[0]
    rows = length // GRID_W
    rb = min(8, rows)
    npair = 2 * NAT_KR - 2
    return pl.pallas_call(
        functools.partial(_nat_kernel, rb=rb, rows=rows),
        grid=(N_NAT, rows // rb),
        in_specs=[pl.BlockSpec((rb * GRID_W, LANES), lambda h, j: (j, PB_NQ // LANES + h)),
                  pl.BlockSpec((length, LANES), lambda h, j: (0, PB_NK // LANES + h)),
                  pl.BlockSpec((length, LANES), lambda h, j: (0, PB_NV // LANES + h)),
                  pl.BlockSpec((lc, LANES), lambda h, j: (0, PB_NK // LANES + h)),
                  pl.BlockSpec((lc, LANES), lambda h, j: (0, PB_NV // LANES + h)),
                  pl.BlockSpec((1, 1, npair, GRID_W, 2 * GRID_W), lambda h, j: (l, h, 0, 0, 0))],
        out_specs=pl.BlockSpec((rb * GRID_W, LANES), lambda h, j: (j, h)),
        out_shape=jax.ShapeDtypeStruct((length, D_NAT), BF16),
        compiler_params=_cparams(("arbitrary", "arbitrary")),
        name="nat",
    )(pb, pb, pb, pbc, pbc, tt)


def _ctx_attn_kernel(q_ref, k_ref, v_ref, o_ref):
    s = _dot_nt(q_ref[...], k_ref[...]) * (HEAD_DIM ** -0.5)
    e = jnp.exp(s - jnp.max(s, axis=-1, keepdims=True))
    den = jnp.sum(e, axis=-1, keepdims=True)
    o_ref[...] = (_dot(e.astype(BF16), v_ref[...]) / den).astype(BF16)


def _ctx_attn(pbc):
    lc = pbc.shape[0]
    return pl.pallas_call(
        _ctx_attn_kernel,
        grid=(N_NAT,),
        in_specs=[pl.BlockSpec((lc, LANES), lambda h: (0, PB_NQ // LANES + h)),
                  pl.BlockSpec((lc, LANES), lambda h: (0, PB_NK // LANES + h)),
                  pl.BlockSpec((lc, LANES), lambda h: (0, PB_NV // LANES + h))],
        out_specs=pl.BlockSpec((lc, LANES), lambda h: (0, h)),
        out_shape=jax.ShapeDtypeStruct((lc, D_NAT), BF16),
        compiler_params=_cparams(("arbitrary",)),
        name="ctx_attn",
    )(pbc, pbc, pbc)


def _mm_post_kernel(*refs, splits, ni, n_total):
    na = len(splits)
    a_refs = refs[:na]
    w_ref, x_ref, gain_ref, gate_ref, o_ref, acc_ref, ss_ref = refs[na:]
    i = pl.program_id(0)
    j = pl.program_id(1)
    slot = i % 2

    @pl.when(i < ni)
    def _():
        acc = None
        off = 0
        for a_ref, kk in zip(a_refs, splits):
            part = _dot(a_ref[...], w_ref[0, off:off + kk, :])
            acc = part if acc is None else acc + part
            off += kk
        acc_ref[slot, j] = acc
        sq = jnp.sum(acc * acc, axis=-1, keepdims=True)
        ss_ref[slot] = jnp.where(j == 0, sq, ss_ref[slot] + sq)

    @pl.when(i > 0)
    def _():
        prev = 1 - slot
        rstd = lax.rsqrt(ss_ref[prev] * (1.0 / n_total) + EPS)
        o_ref[...] = x_ref[...] + gate_ref[...] * (acc_ref[prev, j] * rstd * gain_ref[...])


def _mm_post(a_list, w, l, x, gain, gate):
    m = x.shape[0]
    splits = tuple(a.shape[1] for a in a_list)
    k, n = w.shape[1], w.shape[2]
    bm = min(512, m)
    bn = 512
    ni = m // bm
    nj = n // bn

    def prev_tile(i, j):
        return (jnp.maximum(i - 1, 0), jnp.where(i == 0, 0, j))

    in_specs = [pl.BlockSpec((bm, kk), lambda i, j: (jnp.minimum(i, ni - 1), 0)) for kk in splits]
    in_specs += [pl.BlockSpec((1, k, bn), lambda i, j: (l, 0, jnp.where(i == ni, nj - 1, j))),
                 pl.BlockSpec((bm, bn), prev_tile),
                 pl.BlockSpec((1, bn), lambda i, j: (0, j)),
                 pl.BlockSpec((1, bn), lambda i, j: (0, j))]
    return pl.pallas_call(
        functools.partial(_mm_post_kernel, splits=splits, ni=ni, n_total=n),
        grid=(ni + 1, nj),
        in_specs=in_specs,
        out_specs=pl.BlockSpec((bm, bn), prev_tile),
        out_shape=jax.ShapeDtypeStruct((m, n), F32),
        scratch_shapes=[pltpu.VMEM((2, nj, bm, bn), F32), pltpu.VMEM((2, bm, 1), F32)],
        compiler_params=_cparams(("arbitrary", "arbitrary")),
        name="mm_post",
    )(*a_list, w, x, gain, gate)


FFN_HALO = BF16_SUBLANES
FFN_BN = 512


def _ffn_up_kernel(x_ref, xp_ref, xn_ref, gain_ref, sc_ref, sh_ref, wg_ref, wv_ref, cg_ref, cv_ref, o_ref, h_ref,
                   *, bm, ni):
    i = pl.program_id(0)
    j = pl.program_id(1)
    hl = FFN_HALO

    @pl.when(j == 0)
    def _():
        gain, sc, sh = gain_ref[...], sc_ref[...], sh_ref[...]
        h_ref[hl:hl + bm, :] = _modulate(x_ref[...], gain, sc, sh).astype(BF16)
        hp = _modulate(xp_ref[...], gain, sc, sh)
        hn = _modulate(xn_ref[...], gain, sc, sh)
        h_ref[0:hl, :] = jnp.where(i > 0, hp, 0.0).astype(BF16)
        h_ref[hl + bm:2 * hl + bm, :] = jnp.where(i < ni - 1, hn, 0.0).astype(BF16)

    h = h_ref[...]
    tot = bm + 2 * hl

    def conv(w_ref, cw_ref):
        y = _dot(h, w_ref[0])
        y_prev = pltpu.roll(y, 1, 0)[hl:hl + bm, :]
        y_next = pltpu.roll(y, tot - 1, 0)[hl:hl + bm, :]
        return y_prev * cw_ref[0, 0:1, :] + y[hl:hl + bm, :] * cw_ref[0, 1:2, :] + y_next * cw_ref[0, 2:3, :]

    gate = conv(wg_ref, cg_ref)
    val = conv(wv_ref, cv_ref)
    o_ref[...] = (gate * jax.nn.sigmoid(gate) * val).astype(BF16)


def _ffn_up(x, gain, scale, shift, w, cw, l):
    m, d = x.shape
    bm = min(512, m)
    ni = m // bm
    bn = FFN_BN
    nt = D_FF // bn
    hl = FFN_HALO
    per = bm // hl
    nhalo = m // hl
    vec = pl.BlockSpec((1, d), lambda i, j: (0, 0))
    return pl.pallas_call(
        functools.partial(_ffn_up_kernel, bm=bm, ni=ni),
        grid=(ni, nt),
        in_specs=[pl.BlockSpec((bm, d), lambda i, j: (i, 0)),
                  pl.BlockSpec((hl, d), lambda i, j: (jnp.maximum(i * per - 1, 0), 0)),
                  pl.BlockSpec((hl, d), lambda i, j: (jnp.minimum((i + 1) * per, nhalo - 1), 0)),
                  vec, vec, vec,
                  pl.BlockSpec((1, d, bn), lambda i, j: (l, 0, j)),
                  pl.BlockSpec((1, d, bn), lambda i, j: (l, 0, nt + j)),
                  pl.BlockSpec((1, 3, bn), lambda i, j: (l, 0, j)),
                  pl.BlockSpec((1, 3, bn), lambda i, j: (l, 0, nt + j))],
        out_specs=pl.BlockSpec((bm, bn), lambda i, j: (i, j)),
        out_shape=jax.ShapeDtypeStruct((m, D_FF), BF16),
        scratch_shapes=[pltpu.VMEM((bm + 2 * hl, d), BF16)],
        compiler_params=_cparams(("arbitrary", "arbitrary")),
        name="ffn_up",
    )(x, x, x, gain, scale, shift, w, w, cw, cw)


def _prep_w_in(w):
    sizes = (D_FOURIER, 2 * D_GLA_K, D_GLA, D_GLA, 2 * GLA_RANK, 3 * D_NAT)
    offs = np.concatenate([[0], np.cumsum(sizes)])
    f, qk, v, g, a, nat = [w[:, :, int(offs[t]):int(offs[t + 1])] for t in range(6)]
    w_cat = jnp.concatenate([f, v, nat, qk, g], axis=2).astype(BF16)
    w_a = jnp.pad(a, ((0, 0), (0, 0), (0, LANES - 2 * GLA_RANK))).astype(BF16)
    return w_cat, w_a


def _prep_decay(w_dec, b_dec):
    wd = jnp.zeros((2, LANES, D_GLA_K), F32)
    wd = wd.at[0, 0:GLA_RANK].set(w_dec[0]).at[1, GLA_RANK:2 * GLA_RANK].set(w_dec[1])
    return wd, b_dec.reshape(2, 1, D_GLA_K)


def _state_zero():
    return jnp.zeros((N_GLA // 2, 2 * HEAD_DIM, LANES), F32)


def kernel(x, c, ctx, c_ctx, ada_w, ada_b, norm_mix_pre, norm_mix_post, w_in, gla_w_decay, gla_b_decay,
           gla_norm, fourier_w, nat_rpb, w_out, norm_ffn_pre, norm_ffn_post, ffn_w_up, ffn_w_conv, ffn_w_down):
    depth = ada_w.shape[0]
    d = x.shape[-1]
    xs = x[0]
    cs = ctx[0]
    seq = xs.shape[0]
    lc = cs.shape[0]

    mod = _ada_mod(c, c_ctx, ada_w, ada_b)
    cos_x, sin_x = _rope_tables(seq, True)
    cos_c, sin_c = _rope_tables(lc, False)

    w_cat, w_a = _prep_w_in(w_in)
    w_out_b = w_out.astype(BF16)
    w_up_b = ffn_w_up.astype(BF16)
    w_down_b = ffn_w_down.astype(BF16)
    tt = _nat_bias_table(nat_rpb)

    for l in range(depth):
        last = l == depth - 1
        mx = [mod[l, 0:1, t * d:(t + 1) * d] for t in range(N_MOD)]
        mc = [mod[l, 1:2, t * d:(t + 1) * d] for t in range(N_MOD)]
        vrow = lambda t: t[l].reshape(1, -1)

        wd2, bd2 = _prep_decay(gla_w_decay[l], gla_b_decay[l])
        wc = _four_weights(fourier_w[l])
        gn = vrow(gla_norm)

        pbx, pfx, pax = _inproj(xs, vrow(norm_mix_pre), mx[1], mx[0], w_cat, w_a, l)
        pbc, pfc, pac = _inproj(cs, vrow(norm_mix_pre), mc[1], mc[0], w_cat, w_a, l)

        gla_c, sc_f, sc_b = _gla_bidir(pfc, pac, pbc, cos_c, sin_c, wd2, bd2, gn, _state_zero(), _state_zero())
        gla_x, _, _ = _gla_bidir(pfx, pax, pbx, cos_x, sin_x, wd2, bd2, gn, sc_f, sc_b)
        four_x = _fourier_mix(pbx, wc)
        nat_x = _nat(pbx, pbc, tt, l)
        xs = _mm_post([four_x, gla_x, nat_x], w_out_b, l, xs, vrow(norm_mix_post), mx[2])
        if not last:
            four_c = _fourier_mix(pbc, wc)
            nat_c = _ctx_attn(pbc)
            cs = _mm_post([four_c, gla_c, nat_c], w_out_b, l, cs, vrow(norm_mix_post), mc[2])

        act_x = _ffn_up(xs, vrow(norm_ffn_pre), mx[4], mx[3], w_up_b, ffn_w_conv, l)
        xs = _mm_post([act_x], w_down_b, l, xs, vrow(norm_ffn_post), mx[5])
        if not last:
            act_c = _ffn_up(cs, vrow(norm_ffn_pre), mc[4], mc[3], w_up_b, ffn_w_conv, l)
            cs = _mm_post([act_c], w_down_b, l, cs, vrow(norm_ffn_post), mc[5])
    return xs[None]
```

```python
import functools

import numpy as np
import jax
import jax.numpy as jnp
from jax import lax
from jax.experimental import pallas as pl
from jax.experimental.pallas import tpu as pltpu

F32 = jnp.float32
BF16 = jnp.bfloat16

D_MODEL = 4096
GRID_W = 64
HEAD_DIM = 128
N_FOURIER = 8
N_GLA = 12
N_NAT = 12
D_FOURIER = N_FOURIER * HEAD_DIM
D_GLA = N_GLA * HEAD_DIM
GLA_DK = HEAD_DIM // 2
D_GLA_K = N_GLA * GLA_DK
GLA_RANK = 16
GLA_TAU = 16.0
GLA_CHUNK = 64
D_NAT = N_NAT * HEAD_DIM
NAT_KR = 8
NAT_KC = 16
ROPE_BASE = 10000.0
D_FF = 11 * D_MODEL // 8
N_MOD = 6
EPS = 1e-6
NEG = -1e30

LANES = 128
BF16_SUBLANES = 16
VMEM_LIMIT_MB = 56

PB_F, PB_V, PB_NQ, PB_NK, PB_NV = 0, 1024, 2560, 4096, 5632
N_PB = 7168
PF_Q, PF_K, PF_G = 0, 768, 1536
N_PF = 3072
IN_BN = 1024
GLA_SLABS = 2


def _cparams(sem):
    return pltpu.CompilerParams(dimension_semantics=sem, vmem_limit_bytes=VMEM_LIMIT_MB * 1024 * 1024)


def _dot(a, b):
    return jnp.dot(a, b, preferred_element_type=F32)


def _dot_nt(a, b):
    return lax.dot_general(a, b, (((1,), (1,)), ((), ())), preferred_element_type=F32)


def _dot_tn(a, b):
    return lax.dot_general(a, b, (((0,), (0,)), ((), ())), preferred_element_type=F32)


def _split_bf16(x):
    hi = x.astype(BF16)
    lo = (x - hi.astype(F32)).astype(BF16)
    return hi, lo


def _ada_kernel(cc_ref, w_ref, b_ref, o_ref):
    s = cc_ref[...]
    s = s * jax.nn.sigmoid(s)
    s_hi = s.astype(BF16).astype(F32)
    row = lax.broadcasted_iota(jnp.int32, s.shape, 0)
    lhs = jnp.where(row < 8, s_hi, s - s_hi).astype(BF16)
    w_hi, w_lo = _split_bf16(w_ref[0])
    r = _dot(lhs, w_hi) + _dot(lhs, w_lo)
    o_ref[0] = r[0:8] + r[8:16] + b_ref[0]


def _ada_mod(c, c_ctx, ada_w, ada_b):
    depth, d, n = ada_w.shape
    bn = 512
    cc = jnp.zeros((16, d), F32)
    cc = cc.at[0].set(c[0]).at[1].set(c_ctx).at[8].set(c[0]).at[9].set(c_ctx)
    return pl.pallas_call(
        _ada_kernel,
        grid=(depth, n // bn),
        in_specs=[pl.BlockSpec((16, d), lambda l, j: (0, 0)),
                  pl.BlockSpec((1, d, bn), lambda l, j: (l, 0, j)),
                  pl.BlockSpec((1, 1, bn), lambda l, j: (l, 0, j))],
        out_specs=pl.BlockSpec((1, 8, bn), lambda l, j: (l, 0, j)),
        out_shape=jax.ShapeDtypeStruct((depth, 8, n), F32),
        compiler_params=_cparams(("arbitrary", "arbitrary")),
        name="ada_mod",
    )(cc, ada_w, ada_b.reshape(depth, 1, n))


def _modulate(x, gain, scale, shift):
    ms = jnp.mean(x * x, axis=-1, keepdims=True)
    y = x * lax.rsqrt(ms + EPS) * gain
    return y * (1.0 + scale) + shift


def _inproj_kernel(x_ref, gain_ref, sc_ref, sh_ref, w_ref, wa_ref, ob_ref, of_ref, oa_ref, h_ref, *, nb_tiles):
    j = pl.program_id(1)

    @pl.when(j == 0)
    def _():
        h = _modulate(x_ref[...], gain_ref[...], sc_ref[...], sh_ref[...]).astype(BF16)
        h_ref[...] = h
        oa_ref[...] = _dot(h, wa_ref[0])

    r = _dot(h_ref[...], w_ref[0])

    @pl.when(j < nb_tiles)
    def _():
        ob_ref[...] = r.astype(BF16)

    @pl.when(j >= nb_tiles)
    def _():
        of_ref[...] = r


def _inproj(x, gain, scale, shift, w, wa, l):
    m, d = x.shape
    bm = min(512, m)
    bn = IN_BN
    nb_tiles = N_PB // bn
    nf_tiles = N_PF // bn
    vec = pl.BlockSpec((1, d), lambda i, j: (0, 0))
    return pl.pallas_call(
        functools.partial(_inproj_kernel, nb_tiles=nb_tiles),
        grid=(m // bm, nb_tiles + nf_tiles),
        in_specs=[pl.BlockSpec((bm, d), lambda i, j: (i, 0)), vec, vec, vec,
                  pl.BlockSpec((1, d, bn), lambda i, j: (l, 0, j)),
                  pl.BlockSpec((1, d, LANES), lambda i, j: (l, 0, 0))],
        out_specs=[pl.BlockSpec((bm, bn), lambda i, j: (i, jnp.minimum(j, nb_tiles - 1))),
                   pl.BlockSpec((bm, bn), lambda i, j: (i, jnp.maximum(j - nb_tiles, 0))),
                   pl.BlockSpec((bm, LANES), lambda i, j: (i, 0))],
        out_shape=[jax.ShapeDtypeStruct((m, N_PB), BF16), jax.ShapeDtypeStruct((m, N_PF), F32),
                   jax.ShapeDtypeStruct((m, LANES), F32)],
        scratch_shapes=[pltpu.VMEM((bm, d), BF16)],
        compiler_params=_cparams(("arbitrary", "arbitrary")),
        name="inproj",
    )(x, gain, scale, shift, w, wa)


def _four_w_kernel(cc_ref, sc_ref, fw_ref, o_ref):
    c_hi, c_lo = _split_bf16(cc_ref[...])
    s_hi, s_lo = _split_bf16(sc_ref[...])
    for g in range(N_FOURIER):
        w_hi, w_lo = _split_bf16(fw_ref[g])
        a = _dot(c_hi, w_hi) + _dot(c_hi, w_lo) + _dot(c_lo, w_hi)
        b = _dot(s_hi, w_hi) + _dot(s_hi, w_lo) + _dot(s_lo, w_hi)
        o_ref[g, :, 0:HEAD_DIM] = a
        o_ref[g, :, HEAD_DIM:2 * HEAD_DIM] = -b


def _four_weights(fourier_w_l):
    k = np.arange(HEAD_DIM)
    ang = 2.0 * np.pi * ((k[:, None] * k[None, :]) % HEAD_DIM) / HEAD_DIM
    cc = jnp.asarray(np.cos(ang), F32)
    sc = jnp.asarray(np.sin(ang), F32)
    return pl.pallas_call(
        _four_w_kernel,
        out_shape=jax.ShapeDtypeStruct((N_FOURIER, HEAD_DIM, 2 * HEAD_DIM), F32),
        name="four_w",
    )(cc, sc, fourier_w_l)


def _four_s0_kernel(u_ref, wc_ref, v_ref, *, norm):
    for g in range(N_FOURIER):
        cols = slice(g * HEAD_DIM, (g + 1) * HEAD_DIM)
        y = _dot(u_ref[:, cols], wc_ref[g].astype(BF16)) * norm
        v_ref[0, :, cols] = y[:, 0:HEAD_DIM].astype(BF16)
        v_ref[1, :, cols] = y[:, HEAD_DIM:2 * HEAD_DIM].astype(BF16)


def _four_s1_kernel(m1_ref, v_ref, z_ref):
    z_ref[...] = _dot(m1_ref[...].astype(BF16), v_ref[...]).astype(BF16)


def _four_s2_kernel(t_ref, z_ref, o_ref):
    kb, n2, ch = z_ref.shape[1], z_ref.shape[2], z_ref.shape[3]
    for kk in range(kb):
        z = jnp.concatenate([z_ref[0, kk], z_ref[1, kk]], axis=0)
        o_ref[:, kk * ch:(kk + 1) * ch] = _dot(t_ref[kk].astype(BF16), z).astype(BF16)


@functools.lru_cache(maxsize=None)
def _four_tables(n1, n2):
    length = n1 * n2
    m1 = None
    if n1 > 1:
        j = np.arange(n1)
        ang = 2.0 * np.pi * ((j[:, None] * j[None, :]) % n1) / n1
        cs, ss = np.cos(ang), np.sin(ang)
        m1 = np.block([[cs, ss], [-ss, cs]]).astype(np.float32)
    k1 = np.arange(n1, dtype=np.int64)[:, None, None]
    k2 = np.arange(n2, dtype=np.int64)[None, :, None]
    j2 = np.arange(n2, dtype=np.int64)[None, None, :]
    ang = 2.0 * np.pi * ((j2 * (k1 + n1 * k2)) % length) / length
    t2 = np.concatenate([np.cos(ang), np.sin(ang)], axis=-1).astype(np.float32)
    return m1, t2


def _fourier_mix(pb, wc):
    length = pb.shape[0]
    ch = D_FOURIER
    n2 = length if length <= 512 else 128
    n1 = length // n2
    m1, t2 = _four_tables(n1, n2)
    norm = float(1.0 / np.sqrt(float(length) * HEAD_DIM))
    bm = min(512, length)
    v = pl.pallas_call(
        functools.partial(_four_s0_kernel, norm=norm),
        grid=(length // bm,),
        in_specs=[pl.BlockSpec((bm, ch), lambda i: (i, 0)),
                  pl.BlockSpec((N_FOURIER, HEAD_DIM, 2 * HEAD_DIM), lambda i: (0, 0, 0))],
        out_specs=pl.BlockSpec((2, bm, ch), lambda i: (0, i, 0)),
        out_shape=jax.ShapeDtypeStruct((2, length, ch), BF16),
        compiler_params=_cparams(("arbitrary",)),
        name="four_s0",
    )(pb, wc)
    if n1 > 1:
        ncol = n2 * ch
        bc = min(4096, ncol)
        z = pl.pallas_call(
            _four_s1_kernel,
            grid=(ncol // bc,),
            in_specs=[pl.BlockSpec((2 * n1, 2 * n1), lambda i: (0, 0)),
                      pl.BlockSpec((2 * n1, bc), lambda i: (0, i))],
            out_specs=pl.BlockSpec((2 * n1, bc), lambda i: (0, i)),
            out_shape=jax.ShapeDtypeStruct((2 * n1, ncol), BF16),
            compiler_params=_cparams(("arbitrary",)),
            name="four_s1",
        )(jnp.asarray(m1), v.reshape(2 * n1, ncol))
    else:
        z = v
    z4 = z.reshape(2, n1, n2, ch)
    kb = min(4, n1)
    out = pl.pallas_call(
        _four_s2_kernel,
        grid=(n1 // kb,),
        in_specs=[pl.BlockSpec((kb, n2, 2 * n2), lambda i: (i, 0, 0)),
                  pl.BlockSpec((2, kb, n2, ch), lambda i: (0, i, 0, 0))],
        out_specs=pl.BlockSpec((n2, kb * ch), lambda i: (0, i)),
        out_shape=jax.ShapeDtypeStruct((n2, n1 * ch), BF16),
        compiler_params=_cparams(("arbitrary",)),
        name="four_s2",
    )(jnp.asarray(t2), z4)
    return out.reshape(length, ch)


def _rope_tables(length, use_rope):
    if not use_rope:
        return jnp.ones((length, LANES), F32), jnp.zeros((length, LANES), F32)
    pos = jnp.arange(length)
    prow = (pos // GRID_W).astype(F32)
    pcol = (pos % GRID_W).astype(F32)
    half = GLA_DK // 4
    freqs = ROPE_BASE ** (-jnp.arange(half, dtype=F32) / half)
    ar = prow[:, None] * freqs
    ac = pcol[:, None] * freqs
    cos64 = jnp.concatenate([jnp.cos(ar), jnp.cos(ar), jnp.cos(ac), jnp.cos(ac)], axis=-1)
    sin64 = jnp.concatenate([-jnp.sin(ar), jnp.sin(ar), -jnp.sin(ac), jnp.sin(ac)], axis=-1)
    return jnp.concatenate([cos64, cos64], axis=-1), jnp.concatenate([sin64, sin64], axis=-1)


def _rope(x, cos, sin):
    lane = lax.broadcasted_iota(jnp.int32, x.shape, 1)
    first = (lane % 32) < 16
    partner = jnp.where(first, pltpu.roll(x, LANES - 16, 1), pltpu.roll(x, 16, 1))
    return x * cos + partner * sin


def _gla_kernel(*refs, reverse, nchunk, nblk, finalize):
    if finalize:
        (q_ref, k_ref, a_ref, v_ref, cos_ref, sin_ref, wd_ref, bd_ref, tri_ref, s0_ref,
         of_ref, g_ref, gn_ref, o_ref, sfin_ref, st_ref) = refs
    else:
        (q_ref, k_ref, a_ref, v_ref, cos_ref, sin_ref, wd_ref, bd_ref, tri_ref, s0_ref,
         o_ref, sfin_ref, st_ref) = refs
        of_ref = g_ref = gn_ref = None
    jb = pl.program_id(1)

    @pl.when(jb == 0)
    def _():
        st_ref[...] = s0_ref[...]

    finals = []
    for sl in range(GLA_SLABS):
        one = slice(sl * LANES, (sl + 1) * LANES)
        two = slice(2 * sl * LANES, 2 * (sl + 1) * LANES)
        finals.append(_gla_slab(
            q_ref[:, one], k_ref[:, one], a_ref[...], v_ref, two, cos_ref[...], sin_ref[...],
            wd_ref[:, one], bd_ref[:, one], tri_ref[...], st_ref[sl],
            of_ref, g_ref, gn_ref, o_ref, reverse=reverse, nchunk=nchunk, finalize=finalize))
    for sl in range(GLA_SLABS):
        st_ref[sl] = finals[sl]

    @pl.when(jb == nblk - 1)
    def _():
        for sl in range(GLA_SLABS):
            sfin_ref[sl] = finals[sl]


def _gla_slab(q, k, a, v_ref, two, cos, sin, wd, bd, tri, s, of_ref, g_ref, gn_ref, o_ref, *,
              reverse, nchunk, finalize):
    c = GLA_CHUNK
    z = _dot(a.astype(BF16), wd.astype(BF16)) + bd
    g = (jnp.minimum(z, 0.0) - jnp.log(1.0 + jnp.exp(-jnp.abs(z)))) * (1.0 / GLA_TAU)
    g_hi, g_lo = _split_bf16(g)
    bsum = _dot(tri, jnp.concatenate([g_hi, g_lo], axis=1))
    b = bsum[:, 0:LANES] + bsum[:, LANES:2 * LANES]
    q_dec = _rope(q, cos, sin) * (GLA_DK ** -0.5) * jnp.exp(b)
    k_inv = _rope(k, cos, sin) * jnp.exp(-b)

    lane = lax.broadcasted_iota(jnp.int32, (c, LANES), 1)
    row = lax.broadcasted_iota(jnp.int32, (c, LANES), 0)
    head0 = lane < GLA_DK
    key_pos = lane % c
    causal = (key_pos >= row) if reverse else (key_pos <= row)
    lane2 = lax.broadcasted_iota(jnp.int32, (c, 2 * LANES), 1)
    vhead0 = lane2 < LANES
    srow = lax.broadcasted_iota(jnp.int32, (2 * HEAD_DIM, LANES), 0)
    slane = lax.broadcasted_iota(jnp.int32, (2 * HEAD_DIM, LANES), 1)
    state_mask = (srow < HEAD_DIM) == (slane < GLA_DK)

    order = range(nchunk - 1, -1, -1) if reverse else range(nchunk)
    for ci in order:
        lo = ci * c
        last = lo if reverse else lo + c - 1
        dec = jnp.exp(b[last:last + 1, :])
        qd = q_dec[lo:lo + c, :].astype(BF16)
        ki = k_inv[lo:lo + c, :]
        ke = (ki * dec).astype(BF16)
        zero = jnp.zeros_like(ki)
        kbd = jnp.concatenate([jnp.where(head0, ki, zero), jnp.where(head0, zero, ki)], axis=0).astype(BF16)
        att = _dot_nt(qd, kbd)
        att = jnp.where(causal, att, 0.0).astype(BF16)
        vc = v_ref[lo:lo + c, two]
        vzero = jnp.zeros_like(vc)
        vbd = jnp.concatenate([jnp.where(vhead0, vc, vzero), jnp.where(vhead0, vzero, vc)], axis=0)
        o = _dot(att, vbd) + _dot_nt(qd, s.astype(BF16))
        if finalize:
            o = o + of_ref[lo:lo + c, two]
            gate = g_ref[lo:lo + c, two]
            gate = gate * jax.nn.sigmoid(gate)
            gn = gn_ref[...]
            outs = []
            for h in range(2):
                oh = o[:, h * HEAD_DIM:(h + 1) * HEAD_DIM]
                ms = jnp.mean(oh * oh, axis=-1, keepdims=True)
                outs.append(oh * lax.rsqrt(ms + EPS) * gn)
            o_ref[lo:lo + c, two] = (jnp.concatenate(outs, axis=1) * gate).astype(o_ref.dtype)
        else:
            o_ref[lo:lo + c, two] = o
        ut = _dot_tn(vc, ke)
        s = dec * s + jnp.where(state_mask, ut, 0.0)
    return s


def _gla_tri(tb, reverse):
    t = np.arange(tb)
    same = (t[:, None] // GLA_CHUNK) == (t[None, :] // GLA_CHUNK)
    tri = (t[None, :] >= t[:, None]) if reverse else (t[None, :] <= t[:, None])
    return jnp.asarray((same & tri).astype(np.float32), BF16)


def _gla_pass(pf, pa, pb, cos, sin, wd, bd, s0, reverse, fin=None):
    m = pf.shape[0]
    tb = min(512, m)
    nblk = m // tb
    npair = N_GLA // 2
    finalize = fin is not None

    def rowblk(j):
        return (nblk - 1 - j) if reverse else j

    ns = GLA_SLABS
    w1 = ns * LANES
    w2 = 2 * w1
    in_specs = [
        pl.BlockSpec((tb, w1), lambda h, j: (rowblk(j), PF_Q // w1 + h)),
        pl.BlockSpec((tb, w1), lambda h, j: (rowblk(j), PF_K // w1 + h)),
        pl.BlockSpec((tb, LANES), lambda h, j: (rowblk(j), 0)),
        pl.BlockSpec((tb, w2), lambda h, j: (rowblk(j), PB_V // w2 + h)),
        pl.BlockSpec((tb, LANES), lambda h, j: (rowblk(j), 0)),
        pl.BlockSpec((tb, LANES), lambda h, j: (rowblk(j), 0)),
        pl.BlockSpec((LANES, w1), lambda h, j: (0, h)),
        pl.BlockSpec((1, w1), lambda h, j: (0, h)),
        pl.BlockSpec((tb, tb), lambda h, j: (0, 0)),
        pl.BlockSpec((ns, 2 * HEAD_DIM, LANES), lambda h, j: (h, 0, 0)),
    ]
    args = [pf, pf, pa, pb, cos, sin, wd, bd, _gla_tri(tb, reverse), s0]
    if finalize:
        o_other, gn = fin
        in_specs += [
            pl.BlockSpec((tb, w2), lambda h, j: (rowblk(j), h)),
            pl.BlockSpec((tb, w2), lambda h, j: (rowblk(j), PF_G // w2 + h)),
            pl.BlockSpec((1, HEAD_DIM), lambda h, j: (0, 0)),
        ]
        args += [o_other, pf, gn]
    out_dtype = BF16 if finalize else F32
    return pl.pallas_call(
        functools.partial(_gla_kernel, reverse=reverse, nchunk=tb // GLA_CHUNK, nblk=nblk, finalize=finalize),
        grid=(npair // ns, nblk),
        in_specs=in_specs,
        out_specs=[pl.BlockSpec((tb, w2), lambda h, j: (rowblk(j), h)),
                   pl.BlockSpec((ns, 2 * HEAD_DIM, LANES), lambda h, j: (h, 0, 0))],
        out_shape=[jax.ShapeDtypeStruct((m, D_GLA), out_dtype),
                   jax.ShapeDtypeStruct((npair, 2 * HEAD_DIM, LANES), F32)],
        scratch_shapes=[pltpu.VMEM((ns, 2 * HEAD_DIM, LANES), F32)],
        compiler_params=_cparams(("arbitrary", "arbitrary")),
        name="gla_bwd" if reverse else "gla_fwd",
    )(*args)


def _gla_bidir(pf, pa, pb, cos, sin, wd2, bd2, gn, s0_f, s0_b):
    o_f, s_f = _gla_pass(pf, pa, pb, cos, sin, wd2[0], bd2[0], s0_f, reverse=False)
    out, s_b = _gla_pass(pf, pa, pb, cos, sin, wd2[1], bd2[1], s0_b, reverse=True, fin=(o_f, gn))
    return out, s_f, s_b


@functools.lru_cache(maxsize=None)
def _nat_table_consts():
    qcol = np.arange(GRID_W)
    cstart = np.clip(qcol - NAT_KC // 2, 0, GRID_W - NAT_KC)
    in_win = (qcol[None, :] >= cstart[:, None]) & (qcol[None, :] < cstart[:, None] + NAT_KC)
    col_idx = np.clip(qcol[None, :] - qcol[:, None] + NAT_KC - 1, 0, 2 * NAT_KC - 2)
    width = 2 * NAT_KC
    qc, half, kc = np.meshgrid(qcol, np.arange(2), qcol, indexing="ij")
    pos = (qc * 2 * GRID_W + half * GRID_W + kc).ravel()
    src = (half * width + col_idx[qc, kc]).ravel()
    valid = in_win[qc, kc].ravel()
    expand = np.zeros((2 * width, 2 * GRID_W * GRID_W), np.float32)
    expand[src[valid], pos[valid]] = 1.0
    neg = np.zeros((1, 2 * GRID_W * GRID_W), np.float32)
    neg[0, pos[~valid]] = NEG
    return expand, neg


def _nat_table_kernel(r_ref, e_ref, n_ref, o_ref):
    r = r_ref[...]
    hi = r.astype(BF16)
    rem = r - hi.astype(F32)
    mid = rem.astype(BF16)
    lo = (rem - mid.astype(F32)).astype(BF16)
    e = e_ref[...].astype(BF16)
    o_ref[...] = _dot(hi, e) + _dot(mid, e) + _dot(lo, e) + n_ref[...]


def _nat_bias_table(rpb_all):
    depth, nh = rpb_all.shape[0], rpb_all.shape[1]
    npair = 2 * NAT_KR - 2
    expand, neg = _nat_table_consts()
    rp = jnp.pad(rpb_all.astype(F32), ((0, 0), (0, 0), (0, 0), (0, 1)))
    rows = jnp.concatenate([rp[:, :, 0:npair], rp[:, :, 1:npair + 1]], axis=-1).reshape(depth * nh * npair, -1)
    ncol = expand.shape[1]
    bc = 2048
    out = pl.pallas_call(
        _nat_table_kernel,
        grid=(ncol // bc,),
        in_specs=[pl.BlockSpec(rows.shape, lambda i: (0, 0)),
                  pl.BlockSpec((expand.shape[0], bc), lambda i: (0, i)),
                  pl.BlockSpec((1, bc), lambda i: (0, i))],
        out_specs=pl.BlockSpec((rows.shape[0], bc), lambda i: (0, i)),
        out_shape=jax.ShapeDtypeStruct((rows.shape[0], ncol), F32),
        compiler_params=_cparams(("arbitrary",)),
        name="nat_table",
    )(rows, jnp.asarray(expand), jnp.asarray(neg))
    return out.reshape(depth, nh, npair, GRID_W, 2 * GRID_W)


def _nat_kernel(q_ref, k_ref, v_ref, kc_ref, vc_ref, tt_ref, o_ref, *, rb, rows):
    jb = pl.program_id(1)
    scale = HEAD_DIM ** -0.5
    nloc = NAT_KR * GRID_W
    q_all = q_ref[...]
    s_ctx = _dot_nt(q_all, kc_ref[...]) * scale
    starts = []
    s_rows = []
    for t in range(rb):
        r = jb * rb + t
        rs = jnp.clip(r - NAT_KR // 2, 0, rows - NAT_KR)
        d = r - rs
        start = pl.multiple_of(rs * GRID_W, GRID_W)
        starts.append(start)
        kw = k_ref[pl.ds(start, nloc), :]
        bias = jnp.concatenate([tt_ref[0, 0, NAT_KR - 1 - d + 2 * p] for p in range(NAT_KR // 2)], axis=1)
        s_rows.append(_dot_nt(q_all[t * GRID_W:(t + 1) * GRID_W, :], kw) * scale + bias)
    s_loc = jnp.concatenate(s_rows, axis=0)
    mx = jnp.maximum(jnp.max(s_loc, axis=-1, keepdims=True), jnp.max(s_ctx, axis=-1, keepdims=True))
    e_loc = jnp.exp(s_loc - mx)
    e_ctx = jnp.exp(s_ctx - mx)
    den = jnp.sum(e_loc, axis=-1, keepdims=True) + jnp.sum(e_ctx, axis=-1, keepdims=True)
    e_loc = e_loc.astype(BF16)
    o_ctx = _dot(e_ctx.astype(BF16), vc_ref[...])
    o_rows = [_dot(e_loc[t * GRID_W:(t + 1) * GRID_W, :], v_ref[pl.ds(starts[t], nloc), :]) for t in range(rb)]
    o_ref[...] = ((jnp.concatenate(o_rows, axis=0) + o_ctx) / den).astype(BF16)


def _nat(pb, pbc, tt, l):
    length = pb.shape[0]
    lc = pbc.shape[0]
    rows = length // GRID_W
    rb = min(8, rows)
    npair = 2 * NAT_KR - 2
    return pl.pallas_call(
        functools.partial(_nat_kernel, rb=rb, rows=rows),
        grid=(N_NAT, rows // rb),
        in_specs=[pl.BlockSpec((rb * GRID_W, LANES), lambda h, j: (j, PB_NQ // LANES + h)),
                  pl.BlockSpec((length, LANES), lambda h, j: (0, PB_NK // LANES + h)),
                  pl.BlockSpec((length, LANES), lambda h, j: (0, PB_NV // LANES + h)),
                  pl.BlockSpec((lc, LANES), lambda h, j: (0, PB_NK // LANES + h)),
                  pl.BlockSpec((lc, LANES), lambda h, j: (0, PB_NV // LANES + h)),
                  pl.BlockSpec((1, 1, npair, GRID_W, 2 * GRID_W), lambda h, j: (l, h, 0, 0, 0))],
        out_specs=pl.BlockSpec((rb * GRID_W, LANES), lambda h, j: (j, h)),
        out_shape=jax.ShapeDtypeStruct((length, D_NAT), BF16),
        compiler_params=_cparams(("arbitrary", "arbitrary")),
        name="nat",
    )(pb, pb, pb, pbc, pbc, tt)


def _ctx_attn_kernel(q_ref, k_ref, v_ref, o_ref):
    s = _dot_nt(q_ref[...], k_ref[...]) * (HEAD_DIM ** -0.5)
    e = jnp.exp(s - jnp.max(s, axis=-1, keepdims=True))
    den = jnp.sum(e, axis=-1, keepdims=True)
    o_ref[...] = (_dot(e.astype(BF16), v_ref[...]) / den).astype(BF16)


def _ctx_attn(pbc):
    lc = pbc.shape[0]
    return pl.pallas_call(
        _ctx_attn_kernel,
        grid=(N_NAT,),
        in_specs=[pl.BlockSpec((lc, LANES), lambda h: (0, PB_NQ // LANES + h)),
                  pl.BlockSpec((lc, LANES), lambda h: (0, PB_NK // LANES + h)),
                  pl.BlockSpec((lc, LANES), lambda h: (0, PB_NV // LANES + h))],
        out_specs=pl.BlockSpec((lc, LANES), lambda h: (0, h)),
        out_shape=jax.ShapeDtypeStruct((lc, D_NAT), BF16),
        compiler_params=_cparams(("arbitrary",)),
        name="ctx_attn",
    )(pbc, pbc, pbc)


def _mm_post_kernel(*refs, splits, ni, n_total):
    na = len(splits)
    a_refs = refs[:na]
    w_ref, x_ref, gain_ref, gate_ref, o_ref, acc_ref, ss_ref = refs[na:]
    i = pl.program_id(0)
    j = pl.program_id(1)
    slot = i % 2

    @pl.when(i < ni)
    def _():
        acc = None
        off = 0
        for a_ref, kk in zip(a_refs, splits):
            part = _dot(a_ref[...], w_ref[0, off:off + kk, :])
            acc = part if acc is None else acc + part
            off += kk
        acc_ref[slot, j] = acc
        sq = jnp.sum(acc * acc, axis=-1, keepdims=True)
        ss_ref[slot] = jnp.where(j == 0, sq, ss_ref[slot] + sq)

    @pl.when(i > 0)
    def _():
        prev = 1 - slot
        rstd = lax.rsqrt(ss_ref[prev] * (1.0 / n_total) + EPS)
        o_ref[...] = x_ref[...] + gate_ref[...] * (acc_ref[prev, j] * rstd * gain_ref[...])


def _mm_post(a_list, w, l, x, gain, gate):
    m = x.shape[0]
    splits = tuple(a.shape[1] for a in a_list)
    k, n = w.shape[1], w.shape[2]
    bm = min(512, m)
    bn = 1024 if k <= 4096 else 512
    ni = m // bm
    nj = n // bn

    def prev_tile(i, j):
        return (jnp.maximum(i - 1, 0), jnp.where(i == 0, 0, j))

    in_specs = [pl.BlockSpec((bm, kk), lambda i, j: (jnp.minimum(i, ni - 1), 0)) for kk in splits]
    in_specs += [pl.BlockSpec((1, k, bn), lambda i, j: (l, 0, jnp.where(i == ni, nj - 1, j))),
                 pl.BlockSpec((bm, bn), prev_tile),
                 pl.BlockSpec((1, bn), lambda i, j: (0, j)),
                 pl.BlockSpec((1, bn), lambda i, j: (0, j))]
    return pl.pallas_call(
        functools.partial(_mm_post_kernel, splits=splits, ni=ni, n_total=n),
        grid=(ni + 1, nj),
        in_specs=in_specs,
        out_specs=pl.BlockSpec((bm, bn), prev_tile),
        out_shape=jax.ShapeDtypeStruct((m, n), F32),
        scratch_shapes=[pltpu.VMEM((2, nj, bm, bn), F32), pltpu.VMEM((2, bm, 1), F32)],
        compiler_params=_cparams(("arbitrary", "arbitrary")),
        name="mm_post",
    )(*a_list, w, x, gain, gate)


FFN_HALO = BF16_SUBLANES // 2
FFN_BN = 512


def _ffn_up_kernel(x_ref, xp_ref, xn_ref, gain_ref, sc_ref, sh_ref, wg_ref, wv_ref, cg_ref, cv_ref, o_ref, h_ref,
                   *, bm, ni):
    i = pl.program_id(0)
    j = pl.program_id(1)
    hl = FFN_HALO

    @pl.when(j == 0)
    def _():
        gain, sc, sh = gain_ref[...], sc_ref[...], sh_ref[...]
        h_ref[0:bm, :] = _modulate(x_ref[...], gain, sc, sh).astype(BF16)
        hp = _modulate(xp_ref[...], gain, sc, sh)
        hn = _modulate(xn_ref[...], gain, sc, sh)
        halo = jnp.concatenate([jnp.where(i > 0, hp, 0.0), jnp.where(i < ni - 1, hn, 0.0)], axis=0)
        h_ref[bm:bm + 2 * hl, :] = halo.astype(BF16)

    h = h_ref[...]
    row = lax.broadcasted_iota(jnp.int32, (bm, o_ref.shape[1]), 0)

    def conv(w_ref, cw_ref):
        y = _dot(h, w_ref[0])
        ym = y[0:bm, :]
        y_prev = jnp.where(row == 0, y[bm + hl - 1:bm + hl, :], pltpu.roll(ym, 1, 0))
        y_next = jnp.where(row == bm - 1, y[bm + hl:bm + hl + 1, :], pltpu.roll(ym, bm - 1, 0))
        return y_prev * cw_ref[0, 0:1, :] + ym * cw_ref[0, 1:2, :] + y_next * cw_ref[0, 2:3, :]

    gate = conv(wg_ref, cg_ref)
    val = conv(wv_ref, cv_ref)
    o_ref[...] = (gate * jax.nn.sigmoid(gate) * val).astype(BF16)


def _ffn_up(x, gain, scale, shift, w, cw, l):
    m, d = x.shape
    bm = min(512, m)
    ni = m // bm
    bn = FFN_BN
    nt = D_FF // bn
    hl = FFN_HALO
    per = bm // hl
    nhalo = m // hl
    vec = pl.BlockSpec((1, d), lambda i, j: (0, 0))
    return pl.pallas_call(
        functools.partial(_ffn_up_kernel, bm=bm, ni=ni),
        grid=(ni, nt),
        in_specs=[pl.BlockSpec((bm, d), lambda i, j: (i, 0)),
                  pl.BlockSpec((hl, d), lambda i, j: (jnp.maximum(i * per - 1, 0), 0)),
                  pl.BlockSpec((hl, d), lambda i, j: (jnp.minimum((i + 1) * per, nhalo - 1), 0)),
                  vec, vec, vec,
                  pl.BlockSpec((1, d, bn), lambda i, j: (l, 0, j)),
                  pl.BlockSpec((1, d, bn), lambda i, j: (l, 0, nt + j)),
                  pl.BlockSpec((1, 3, bn), lambda i, j: (l, 0, j)),
                  pl.BlockSpec((1, 3, bn), lambda i, j: (l, 0, nt + j))],
        out_specs=pl.BlockSpec((bm, bn), lambda i, j: (i, j)),
        out_shape=jax.ShapeDtypeStruct((m, D_FF), BF16),
        scratch_shapes=[pltpu.VMEM((bm + 2 * hl, d), BF16)],
        compiler_params=_cparams(("arbitrary", "arbitrary")),
        name="ffn_up",
    )(x, x, x, gain, scale, shift, w, w, cw, cw)


def _prep_w_in(w):
    sizes = (D_FOURIER, 2 * D_GLA_K, D_GLA, D_GLA, 2 * GLA_RANK, 3 * D_NAT)
    offs = np.concatenate([[0], np.cumsum(sizes)])
    f, qk, v, g, a, nat = [w[:, :, int(offs[t]):int(offs[t + 1])] for t in range(6)]
    w_cat = jnp.concatenate([f, v, nat, qk, g], axis=2).astype(BF16)
    w_a = jnp.pad(a, ((0, 0), (0, 0), (0, LANES - 2 * GLA_RANK))).astype(BF16)
    return w_cat, w_a


def _prep_decay(w_dec, b_dec):
    wd = jnp.zeros((2, LANES, D_GLA_K), F32)
    wd = wd.at[0, 0:GLA_RANK].set(w_dec[0]).at[1, GLA_RANK:2 * GLA_RANK].set(w_dec[1])
    return wd, b_dec.reshape(2, 1, D_GLA_K)


def _state_zero():
    return jnp.zeros((N_GLA // 2, 2 * HEAD_DIM, LANES), F32)


def kernel(x, c, ctx, c_ctx, ada_w, ada_b, norm_mix_pre, norm_mix_post, w_in, gla_w_decay, gla_b_decay,
           gla_norm, fourier_w, nat_rpb, w_out, norm_ffn_pre, norm_ffn_post, ffn_w_up, ffn_w_conv, ffn_w_down):
    depth = ada_w.shape[0]
    d = x.shape[-1]
    xs = x[0]
    cs = ctx[0]
    seq = xs.shape[0]
    lc = cs.shape[0]

    mod = _ada_mod(c, c_ctx, ada_w, ada_b)
    cos_x, sin_x = _rope_tables(seq, True)
    cos_c, sin_c = _rope_tables(lc, False)

    w_cat, w_a = _prep_w_in(w_in)
    w_out_b = w_out.astype(BF16)
    w_up_b = ffn_w_up.astype(BF16)
    w_down_b = ffn_w_down.astype(BF16)
    tt = _nat_bias_table(nat_rpb)

    for l in range(depth):
        last = l == depth - 1
        mx = [mod[l, 0:1, t * d:(t + 1) * d] for t in range(N_MOD)]
        mc = [mod[l, 1:2, t * d:(t + 1) * d] for t in range(N_MOD)]
        vrow = lambda t: t[l].reshape(1, -1)

        wd2, bd2 = _prep_decay(gla_w_decay[l], gla_b_decay[l])
        wc = _four_weights(fourier_w[l])
        gn = vrow(gla_norm)

        pbx, pfx, pax = _inproj(xs, vrow(norm_mix_pre), mx[1], mx[0], w_cat, w_a, l)
        pbc, pfc, pac = _inproj(cs, vrow(norm_mix_pre), mc[1], mc[0], w_cat, w_a, l)

        gla_c, sc_f, sc_b = _gla_bidir(pfc, pac, pbc, cos_c, sin_c, wd2, bd2, gn, _state_zero(), _state_zero())
        gla_x, _, _ = _gla_bidir(pfx, pax, pbx, cos_x, sin_x, wd2, bd2, gn, sc_f, sc_b)
        four_x = _fourier_mix(pbx, wc)
        nat_x = _nat(pbx, pbc, tt, l)
        xs = _mm_post([four_x, gla_x, nat_x], w_out_b, l, xs, vrow(norm_mix_post), mx[2])
        if not last:
            four_c = _fourier_mix(pbc, wc)
            nat_c = _ctx_attn(pbc)
            cs = _mm_post([four_c, gla_c, nat_c], w_out_b, l, cs, vrow(norm_mix_post), mc[2])

        act_x = _ffn_up(xs, vrow(norm_ffn_pre), mx[4], mx[3], w_up_b, ffn_w_conv, l)
        xs = _mm_post([act_x], w_down_b, l, xs, vrow(norm_ffn_post), mx[5])
        if not last:
            act_c = _ffn_up(cs, vrow(norm_ffn_pre), mc[4], mc[3], w_up_b, ffn_w_conv, l)
            cs = _mm_post([act_c], w_down_b, l, cs, vrow(norm_ffn_post), mc[5])
    return xs[None]
```

```python
import functools

import numpy as np
import jax
import jax.numpy as jnp
from jax import lax
from jax.experimental import pallas as pl
from jax.experimental.pallas import tpu as pltpu

F32 = jnp.float32
BF16 = jnp.bfloat16

D_MODEL = 4096
GRID_W = 64
HEAD_DIM = 128
N_FOURIER = 8
N_GLA = 12
N_NAT = 12
D_FOURIER = N_FOURIER * HEAD_DIM
D_GLA = N_GLA * HEAD_DIM
GLA_DK = HEAD_DIM // 2
D_GLA_K = N_GLA * GLA_DK
GLA_RANK = 16
GLA_TAU = 16.0
GLA_CHUNK = 64
D_NAT = N_NAT * HEAD_DIM
NAT_KR = 8
NAT_KC = 16
ROPE_BASE = 10000.0
D_FF = 11 * D_MODEL // 8
N_MOD = 6
EPS = 1e-6
NEG = -1e30

LANES = 128
BF16_SUBLANES = 16
VMEM_LIMIT_MB = 56

PB_F, PB_V, PB_NQ, PB_NK, PB_NV = 0, 1024, 2560, 4096, 5632
N_PB = 7168
PF_Q, PF_K, PF_G = 0, 768, 1536
N_PF = 3072
IN_BN = 1024
GLA_SLABS = 2


def _cparams(sem):
    return pltpu.CompilerParams(dimension_semantics=sem, vmem_limit_bytes=VMEM_LIMIT_MB * 1024 * 1024)


def _dot(a, b):
    return jnp.dot(a, b, preferred_element_type=F32)


def _dot_nt(a, b):
    return lax.dot_general(a, b, (((1,), (1,)), ((), ())), preferred_element_type=F32)


def _dot_tn(a, b):
    return lax.dot_general(a, b, (((0,), (0,)), ((), ())), preferred_element_type=F32)


def _split_bf16(x):
    hi = x.astype(BF16)
    lo = (x - hi.astype(F32)).astype(BF16)
    return hi, lo


def _ada_kernel(cc_ref, w_ref, b_ref, o_ref):
    s = cc_ref[...]
    s = s * jax.nn.sigmoid(s)
    s_hi = s.astype(BF16).astype(F32)
    row = lax.broadcasted_iota(jnp.int32, s.shape, 0)
    lhs = jnp.where(row < 8, s_hi, s - s_hi).astype(BF16)
    w_hi, w_lo = _split_bf16(w_ref[0])
    r = _dot(lhs, w_hi) + _dot(lhs, w_lo)
    o_ref[0] = r[0:8] + r[8:16] + b_ref[0]


def _ada_mod(c, c_ctx, ada_w, ada_b):
    depth, d, n = ada_w.shape
    bn = 512
    cc = jnp.zeros((16, d), F32)
    cc = cc.at[0].set(c[0]).at[1].set(c_ctx).at[8].set(c[0]).at[9].set(c_ctx)
    return pl.pallas_call(
        _ada_kernel,
        grid=(depth, n // bn),
        in_specs=[pl.BlockSpec((16, d), lambda l, j: (0, 0)),
                  pl.BlockSpec((1, d, bn), lambda l, j: (l, 0, j)),
                  pl.BlockSpec((1, 1, bn), lambda l, j: (l, 0, j))],
        out_specs=pl.BlockSpec((1, 8, bn), lambda l, j: (l, 0, j)),
        out_shape=jax.ShapeDtypeStruct((depth, 8, n), F32),
        compiler_params=_cparams(("arbitrary", "arbitrary")),
        name="ada_mod",
    )(cc, ada_w, ada_b.reshape(depth, 1, n))


def _modulate(x, gain, scale, shift):
    ms = jnp.mean(x * x, axis=-1, keepdims=True)
    y = x * lax.rsqrt(ms + EPS) * gain
    return y * (1.0 + scale) + shift


CAST_ROWS = BF16_SUBLANES


def _cast_specs(srcs, out_widths, nj, nsteps):
    in_specs, out_specs, out_shapes = [], [], []
    for (s, l), widths in zip(srcs, out_widths):
        nrow, ncol = s.shape[1], s.shape[2]
        rows = next(r for r in range(CAST_ROWS, nrow + 1, CAST_ROWS) if nrow % r == 0 and nrow // r <= nsteps)
        nchunk = nrow // rows

        def chunk(i, j, nchunk=nchunk):
            return jnp.minimum(i * nj + j, nchunk - 1)

        in_specs.append(pl.BlockSpec((1, rows, ncol), lambda i, j, l=l, chunk=chunk: (l, chunk(i, j), 0)))
        for wd in widths:
            out_specs.append(pl.BlockSpec((rows, wd), lambda i, j, chunk=chunk: (chunk(i, j), 0)))
            out_shapes.append(jax.ShapeDtypeStruct((nrow, wd), BF16))
    return in_specs, out_specs, out_shapes


def _cast_plain(src_refs, dst_refs):
    for s_ref, d_ref in zip(src_refs, dst_refs):
        d_ref[...] = s_ref[0].astype(BF16)


def _inproj_kernel(*refs, nb_tiles, ncast):
    x_ref, gain_ref, sc_ref, sh_ref, w_ref, wa_ref = refs[:6]
    cast_src = refs[6:6 + ncast]
    ob_ref, of_ref, oa_ref = refs[6 + ncast:9 + ncast]
    cast_dst = refs[9 + ncast:9 + 2 * ncast]
    h_ref = refs[9 + 2 * ncast]
    j = pl.program_id(1)

    @pl.when(j == 0)
    def _():
        h = _modulate(x_ref[...], gain_ref[...], sc_ref[...], sh_ref[...]).astype(BF16)
        h_ref[...] = h
        oa_ref[...] = _dot(h, wa_ref[...])

    r = _dot(h_ref[...], w_ref[...])
    _cast_plain(cast_src, cast_dst)

    @pl.when(j < nb_tiles)
    def _():
        ob_ref[...] = r.astype(BF16)

    @pl.when(j >= nb_tiles)
    def _():
        of_ref[...] = r


def _inproj(x, gain, scale, shift, w, wa, cast=()):
    m, d = x.shape
    bm = min(512, m)
    bn = IN_BN
    nb_tiles = N_PB // bn
    nf_tiles = N_PF // bn
    nj = nb_tiles + nf_tiles
    vec = pl.BlockSpec((1, d), lambda i, j: (0, 0))
    c_in, c_out, c_shapes = _cast_specs(cast, [(s.shape[2],) for s, _ in cast], nj, (m // bm) * nj)
    return pl.pallas_call(
        functools.partial(_inproj_kernel, nb_tiles=nb_tiles, ncast=len(cast)),
        grid=(m // bm, nj),
        in_specs=[pl.BlockSpec((bm, d), lambda i, j: (i, 0)), vec, vec, vec,
                  pl.BlockSpec((d, bn), lambda i, j: (0, j)),
                  pl.BlockSpec((d, LANES), lambda i, j: (0, 0))] + c_in,
        out_specs=[pl.BlockSpec((bm, bn), lambda i, j: (i, jnp.minimum(j, nb_tiles - 1))),
                   pl.BlockSpec((bm, bn), lambda i, j: (i, jnp.maximum(j - nb_tiles, 0))),
                   pl.BlockSpec((bm, LANES), lambda i, j: (i, 0))] + c_out,
        out_shape=[jax.ShapeDtypeStruct((m, N_PB), BF16), jax.ShapeDtypeStruct((m, N_PF), F32),
                   jax.ShapeDtypeStruct((m, LANES), F32)] + c_shapes,
        scratch_shapes=[pltpu.VMEM((bm, d), BF16)],
        compiler_params=_cparams(("arbitrary", "arbitrary")),
        name="inproj",
    )(x, gain, scale, shift, w, wa, *[s for s, _ in cast])


def _four_w_kernel(cc_ref, sc_ref, fw_ref, o_ref):
    c_hi, c_lo = _split_bf16(cc_ref[...])
    s_hi, s_lo = _split_bf16(sc_ref[...])
    for g in range(N_FOURIER):
        w_hi, w_lo = _split_bf16(fw_ref[g])
        a = _dot(c_hi, w_hi) + _dot(c_hi, w_lo) + _dot(c_lo, w_hi)
        b = _dot(s_hi, w_hi) + _dot(s_hi, w_lo) + _dot(s_lo, w_hi)
        o_ref[g, :, 0:HEAD_DIM] = a
        o_ref[g, :, HEAD_DIM:2 * HEAD_DIM] = -b


def _four_weights(fourier_w_l):
    k = np.arange(HEAD_DIM)
    ang = 2.0 * np.pi * ((k[:, None] * k[None, :]) % HEAD_DIM) / HEAD_DIM
    cc = jnp.asarray(np.cos(ang), F32)
    sc = jnp.asarray(np.sin(ang), F32)
    return pl.pallas_call(
        _four_w_kernel,
        out_shape=jax.ShapeDtypeStruct((N_FOURIER, HEAD_DIM, 2 * HEAD_DIM), F32),
        name="four_w",
    )(cc, sc, fourier_w_l)


def _four_s0_kernel(u_ref, wc_ref, v_ref, *, norm):
    for g in range(N_FOURIER):
        cols = slice(g * HEAD_DIM, (g + 1) * HEAD_DIM)
        y = _dot(u_ref[:, cols], wc_ref[g].astype(BF16)) * norm
        v_ref[0, :, cols] = y[:, 0:HEAD_DIM].astype(BF16)
        v_ref[1, :, cols] = y[:, HEAD_DIM:2 * HEAD_DIM].astype(BF16)


def _four_s1_kernel(m1_ref, v_ref, z_ref):
    z_ref[...] = _dot(m1_ref[...].astype(BF16), v_ref[...]).astype(BF16)


def _four_s2_kernel(t_ref, z_ref, o_ref):
    kb, n2, ch = z_ref.shape[1], z_ref.shape[2], z_ref.shape[3]
    for kk in range(kb):
        z = jnp.concatenate([z_ref[0, kk], z_ref[1, kk]], axis=0)
        o_ref[:, kk * ch:(kk + 1) * ch] = _dot(t_ref[kk].astype(BF16), z).astype(BF16)


@functools.lru_cache(maxsize=None)
def _four_tables(n1, n2):
    length = n1 * n2
    m1 = None
    if n1 > 1:
        j = np.arange(n1)
        ang = 2.0 * np.pi * ((j[:, None] * j[None, :]) % n1) / n1
        cs, ss = np.cos(ang), np.sin(ang)
        m1 = np.block([[cs, ss], [-ss, cs]]).astype(np.float32)
    k1 = np.arange(n1, dtype=np.int64)[:, None, None]
    k2 = np.arange(n2, dtype=np.int64)[None, :, None]
    j2 = np.arange(n2, dtype=np.int64)[None, None, :]
    ang = 2.0 * np.pi * ((j2 * (k1 + n1 * k2)) % length) / length
    t2 = np.concatenate([np.cos(ang), np.sin(ang)], axis=-1).astype(np.float32)
    return m1, t2


def _fourier_mix(pb, wc):
    length = pb.shape[0]
    ch = D_FOURIER
    n2 = length if length <= 512 else 128
    n1 = length // n2
    m1, t2 = _four_tables(n1, n2)
    norm = float(1.0 / np.sqrt(float(length) * HEAD_DIM))
    bm = min(512, length)
    v = pl.pallas_call(
        functools.partial(_four_s0_kernel, norm=norm),
        grid=(length // bm,),
        in_specs=[pl.BlockSpec((bm, ch), lambda i: (i, 0)),
                  pl.BlockSpec((N_FOURIER, HEAD_DIM, 2 * HEAD_DIM), lambda i: (0, 0, 0))],
        out_specs=pl.BlockSpec((2, bm, ch), lambda i: (0, i, 0)),
        out_shape=jax.ShapeDtypeStruct((2, length, ch), BF16),
        compiler_params=_cparams(("arbitrary",)),
        name="four_s0",
    )(pb, wc)
    if n1 > 1:
        ncol = n2 * ch
        bc = min(4096, ncol)
        z = pl.pallas_call(
            _four_s1_kernel,
            grid=(ncol // bc,),
            in_specs=[pl.BlockSpec((2 * n1, 2 * n1), lambda i: (0, 0)),
                      pl.BlockSpec((2 * n1, bc), lambda i: (0, i))],
            out_specs=pl.BlockSpec((2 * n1, bc), lambda i: (0, i)),
            out_shape=jax.ShapeDtypeStruct((2 * n1, ncol), BF16),
            compiler_params=_cparams(("arbitrary",)),
            name="four_s1",
        )(jnp.asarray(m1), v.reshape(2 * n1, ncol))
    else:
        z = v
    z4 = z.reshape(2, n1, n2, ch)
    kb = min(4, n1)
    out = pl.pallas_call(
        _four_s2_kernel,
        grid=(n1 // kb,),
        in_specs=[pl.BlockSpec((kb, n2, 2 * n2), lambda i: (i, 0, 0)),
                  pl.BlockSpec((2, kb, n2, ch), lambda i: (0, i, 0, 0))],
        out_specs=pl.BlockSpec((n2, kb * ch), lambda i: (0, i)),
        out_shape=jax.ShapeDtypeStruct((n2, n1 * ch), BF16),
        compiler_params=_cparams(("arbitrary",)),
        name="four_s2",
    )(jnp.asarray(t2), z4)
    return out.reshape(length, ch)


def _rope_tables(length, use_rope):
    if not use_rope:
        return jnp.ones((length, LANES), F32), jnp.zeros((length, LANES), F32)
    pos = jnp.arange(length)
    prow = (pos // GRID_W).astype(F32)
    pcol = (pos % GRID_W).astype(F32)
    half = GLA_DK // 4
    freqs = ROPE_BASE ** (-jnp.arange(half, dtype=F32) / half)
    ar = prow[:, None] * freqs
    ac = pcol[:, None] * freqs
    cos64 = jnp.concatenate([jnp.cos(ar), jnp.cos(ar), jnp.cos(ac), jnp.cos(ac)], axis=-1)
    sin64 = jnp.concatenate([-jnp.sin(ar), jnp.sin(ar), -jnp.sin(ac), jnp.sin(ac)], axis=-1)
    return jnp.concatenate([cos64, cos64], axis=-1), jnp.concatenate([sin64, sin64], axis=-1)


def _rope(x, cos, sin):
    lane = lax.broadcasted_iota(jnp.int32, x.shape, 1)
    first = (lane % 32) < 16
    partner = jnp.where(first, pltpu.roll(x, LANES - 16, 1), pltpu.roll(x, 16, 1))
    return x * cos + partner * sin


def _gla_kernel(*refs, reverse, nchunk, nblk, finalize):
    if finalize:
        (q_ref, k_ref, a_ref, v_ref, cos_ref, sin_ref, wd_ref, bd_ref, tri_ref, s0_ref,
         of_ref, g_ref, gn_ref, o_ref, sfin_ref, st_ref) = refs
    else:
        (q_ref, k_ref, a_ref, v_ref, cos_ref, sin_ref, wd_ref, bd_ref, tri_ref, s0_ref,
         o_ref, sfin_ref, st_ref) = refs
        of_ref = g_ref = gn_ref = None
    jb = pl.program_id(1)

    @pl.when(jb == 0)
    def _():
        st_ref[...] = s0_ref[...]

    finals = []
    for sl in range(GLA_SLABS):
        one = slice(sl * LANES, (sl + 1) * LANES)
        two = slice(2 * sl * LANES, 2 * (sl + 1) * LANES)
        finals.append(_gla_slab(
            q_ref[:, one], k_ref[:, one], a_ref[...], v_ref, two, cos_ref[...], sin_ref[...],
            wd_ref[:, one], bd_ref[:, one], tri_ref[...], st_ref[sl],
            of_ref, g_ref, gn_ref, o_ref, reverse=reverse, nchunk=nchunk, finalize=finalize))
    for sl in range(GLA_SLABS):
        st_ref[sl] = finals[sl]

    @pl.when(jb == nblk - 1)
    def _():
        for sl in range(GLA_SLABS):
            sfin_ref[sl] = finals[sl]


def _gla_slab(q, k, a, v_ref, two, cos, sin, wd, bd, tri, s, of_ref, g_ref, gn_ref, o_ref, *,
              reverse, nchunk, finalize):
    c = GLA_CHUNK
    z = _dot(a.astype(BF16), wd.astype(BF16)) + bd
    g = (jnp.minimum(z, 0.0) - jnp.log(1.0 + jnp.exp(-jnp.abs(z)))) * (1.0 / GLA_TAU)
    g_hi, g_lo = _split_bf16(g)
    bsum = _dot(tri, jnp.concatenate([g_hi, g_lo], axis=1))
    b = bsum[:, 0:LANES] + bsum[:, LANES:2 * LANES]
    q_dec = _rope(q, cos, sin) * (GLA_DK ** -0.5) * jnp.exp(b)
    k_inv = _rope(k, cos, sin) * jnp.exp(-b)

    lane = lax.broadcasted_iota(jnp.int32, (c, LANES), 1)
    row = lax.broadcasted_iota(jnp.int32, (c, LANES), 0)
    head0 = lane < GLA_DK
    key_pos = lane % c
    causal = (key_pos >= row) if reverse else (key_pos <= row)
    lane2 = lax.broadcasted_iota(jnp.int32, (c, 2 * LANES), 1)
    vhead0 = lane2 < LANES
    srow = lax.broadcasted_iota(jnp.int32, (2 * HEAD_DIM, LANES), 0)
    slane = lax.broadcasted_iota(jnp.int32, (2 * HEAD_DIM, LANES), 1)
    state_mask = (srow < HEAD_DIM) == (slane < GLA_DK)

    order = range(nchunk - 1, -1, -1) if reverse else range(nchunk)
    for ci in order:
        lo = ci * c
        last = lo if reverse else lo + c - 1
        dec = jnp.exp(b[last:last + 1, :])
        qd = q_dec[lo:lo + c, :].astype(BF16)
        ki = k_inv[lo:lo + c, :]
        ke = (ki * dec).astype(BF16)
        zero = jnp.zeros_like(ki)
        kbd = jnp.concatenate([jnp.where(head0, ki, zero), jnp.where(head0, zero, ki)], axis=0).astype(BF16)
        att = _dot_nt(qd, kbd)
        att = jnp.where(causal, att, 0.0).astype(BF16)
        vc = v_ref[lo:lo + c, two]
        vzero = jnp.zeros_like(vc)
        vbd = jnp.concatenate([jnp.where(vhead0, vc, vzero), jnp.where(vhead0, vzero, vc)], axis=0)
        o = _dot(att, vbd) + _dot_nt(qd, s.astype(BF16))
        if finalize:
            o = o + of_ref[lo:lo + c, two]
            gate = g_ref[lo:lo + c, two]
            gate = gate * jax.nn.sigmoid(gate)
            gn = gn_ref[...]
            outs = []
            for h in range(2):
                oh = o[:, h * HEAD_DIM:(h + 1) * HEAD_DIM]
                ms = jnp.mean(oh * oh, axis=-1, keepdims=True)
                outs.append(oh * lax.rsqrt(ms + EPS) * gn)
            o_ref[lo:lo + c, two] = (jnp.concatenate(outs, axis=1) * gate).astype(o_ref.dtype)
        else:
            o_ref[lo:lo + c, two] = o
        ut = _dot_tn(vc, ke)
        s = dec * s + jnp.where(state_mask, ut, 0.0)
    return s


def _gla_tri(tb, reverse):
    t = np.arange(tb)
    same = (t[:, None] // GLA_CHUNK) == (t[None, :] // GLA_CHUNK)
    tri = (t[None, :] >= t[:, None]) if reverse else (t[None, :] <= t[:, None])
    return jnp.asarray((same & tri).astype(np.float32), BF16)


def _gla_pass(pf, pa, pb, cos, sin, wd, bd, s0, reverse, fin=None):
    m = pf.shape[0]
    tb = min(512, m)
    nblk = m // tb
    npair = N_GLA // 2
    finalize = fin is not None

    def rowblk(j):
        return (nblk - 1 - j) if reverse else j

    ns = GLA_SLABS
    w1 = ns * LANES
    w2 = 2 * w1
    in_specs = [
        pl.BlockSpec((tb, w1), lambda h, j: (rowblk(j), PF_Q // w1 + h)),
        pl.BlockSpec((tb, w1), lambda h, j: (rowblk(j), PF_K // w1 + h)),
        pl.BlockSpec((tb, LANES), lambda h, j: (rowblk(j), 0)),
        pl.BlockSpec((tb, w2), lambda h, j: (rowblk(j), PB_V // w2 + h)),
        pl.BlockSpec((tb, LANES), lambda h, j: (rowblk(j), 0)),
        pl.BlockSpec((tb, LANES), lambda h, j: (rowblk(j), 0)),
        pl.BlockSpec((LANES, w1), lambda h, j: (0, h)),
        pl.BlockSpec((1, w1), lambda h, j: (0, h)),
        pl.BlockSpec((tb, tb), lambda h, j: (0, 0)),
        pl.BlockSpec((ns, 2 * HEAD_DIM, LANES), lambda h, j: (h, 0, 0)),
    ]
    args = [pf, pf, pa, pb, cos, sin, wd, bd, _gla_tri(tb, reverse), s0]
    if finalize:
        o_other, gn = fin
        in_specs += [
            pl.BlockSpec((tb, w2), lambda h, j: (rowblk(j), h)),
            pl.BlockSpec((tb, w2), lambda h, j: (rowblk(j), PF_G // w2 + h)),
            pl.BlockSpec((1, HEAD_DIM), lambda h, j: (0, 0)),
        ]
        args += [o_other, pf, gn]
    out_dtype = BF16 if finalize else F32
    return pl.pallas_call(
        functools.partial(_gla_kernel, reverse=reverse, nchunk=tb // GLA_CHUNK, nblk=nblk, finalize=finalize),
        grid=(npair // ns, nblk),
        in_specs=in_specs,
        out_specs=[pl.BlockSpec((tb, w2), lambda h, j: (rowblk(j), h)),
                   pl.BlockSpec((ns, 2 * HEAD_DIM, LANES), lambda h, j: (h, 0, 0))],
        out_shape=[jax.ShapeDtypeStruct((m, D_GLA), out_dtype),
                   jax.ShapeDtypeStruct((npair, 2 * HEAD_DIM, LANES), F32)],
        scratch_shapes=[pltpu.VMEM((ns, 2 * HEAD_DIM, LANES), F32)],
        compiler_params=_cparams(("arbitrary", "arbitrary")),
        name="gla_bwd" if reverse else "gla_fwd",
    )(*args)


def _gla_bidir(pf, pa, pb, cos, sin, wd2, bd2, gn, s0_f, s0_b):
    o_f, s_f = _gla_pass(pf, pa, pb, cos, sin, wd2[0], bd2[0], s0_f, reverse=False)
    out, s_b = _gla_pass(pf, pa, pb, cos, sin, wd2[1], bd2[1], s0_b, reverse=True, fin=(o_f, gn))
    return out, s_f, s_b


@functools.lru_cache(maxsize=None)
def _nat_table_consts():
    qcol = np.arange(GRID_W)
    cstart = np.clip(qcol - NAT_KC // 2, 0, GRID_W - NAT_KC)
    in_win = (qcol[None, :] >= cstart[:, None]) & (qcol[None, :] < cstart[:, None] + NAT_KC)
    col_idx = np.clip(qcol[None, :] - qcol[:, None] + NAT_KC - 1, 0, 2 * NAT_KC - 2)
    width = 2 * NAT_KC
    qc, half, kc = np.meshgrid(qcol, np.arange(2), qcol, indexing="ij")
    pos = (qc * 2 * GRID_W + half * GRID_W + kc).ravel()
    src = (half * width + col_idx[qc, kc]).ravel()
    valid = in_win[qc, kc].ravel()
    expand = np.zeros((2 * width, 2 * GRID_W * GRID_W), np.float32)
    expand[src[valid], pos[valid]] = 1.0
    neg = np.zeros((1, 2 * GRID_W * GRID_W), np.float32)
    neg[0, pos[~valid]] = NEG
    return expand, neg


def _nat_table_kernel(r_ref, e_ref, n_ref, o_ref):
    r = r_ref[...]
    hi = r.astype(BF16)
    rem = r - hi.astype(F32)
    mid = rem.astype(BF16)
    lo = (rem - mid.astype(F32)).astype(BF16)
    e = e_ref[...].astype(BF16)
    o_ref[...] = _dot(hi, e) + _dot(mid, e) + _dot(lo, e) + n_ref[...]


def _nat_bias_table(rpb_all):
    depth, nh = rpb_all.shape[0], rpb_all.shape[1]
    npair = 2 * NAT_KR - 2
    expand, neg = _nat_table_consts()
    rp = jnp.pad(rpb_all.astype(F32), ((0, 0), (0, 0), (0, 0), (0, 1)))
    rows = jnp.concatenate([rp[:, :, 0:npair], rp[:, :, 1:npair + 1]], axis=-1).reshape(depth * nh * npair, -1)
    ncol = expand.shape[1]
    bc = 2048
    out = pl.pallas_call(
        _nat_table_kernel,
        grid=(ncol // bc,),
        in_specs=[pl.BlockSpec(rows.shape, lambda i: (0, 0)),
                  pl.BlockSpec((expand.shape[0], bc), lambda i: (0, i)),
                  pl.BlockSpec((1, bc), lambda i: (0, i))],
        out_specs=pl.BlockSpec((rows.shape[0], bc), lambda i: (0, i)),
        out_shape=jax.ShapeDtypeStruct((rows.shape[0], ncol), F32),
        compiler_params=_cparams(("arbitrary",)),
        name="nat_table",
    )(rows, jnp.asarray(expand), jnp.asarray(neg))
    return out.reshape(depth, nh, npair, GRID_W, 2 * GRID_W)


def _nat_kernel(q_ref, k_ref, v_ref, kc_ref, vc_ref, tt_ref, o_ref, *, rb, rows):
    jb = pl.program_id(1)
    scale = HEAD_DIM ** -0.5
    nloc = NAT_KR * GRID_W
    q_all = q_ref[...]
    s_ctx = _dot_nt(q_all, kc_ref[...]) * scale
    starts = []
    s_rows = []
    for t in range(rb):
        r = jb * rb + t
        rs = jnp.clip(r - NAT_KR // 2, 0, rows - NAT_KR)
        d = r - rs
        start = pl.multiple_of(rs * GRID_W, GRID_W)
        starts.append(start)
        kw = k_ref[pl.ds(start, nloc), :]
        bias = jnp.concatenate([tt_ref[0, 0, NAT_KR - 1 - d + 2 * p] for p in range(NAT_KR // 2)], axis=1)
        s_rows.append(_dot_nt(q_all[t * GRID_W:(t + 1) * GRID_W, :], kw) * scale + bias)
    s_loc = jnp.concatenate(s_rows, axis=0)
    mx = jnp.maximum(jnp.max(s_loc, axis=-1, keepdims=True), jnp.max(s_ctx, axis=-1, keepdims=True))
    e_loc = jnp.exp(s_loc - mx)
    e_ctx = jnp.exp(s_ctx - mx)
    den = jnp.sum(e_loc, axis=-1, keepdims=True) + jnp.sum(e_ctx, axis=-1, keepdims=True)
    e_loc = e_loc.astype(BF16)
    o_ctx = _dot(e_ctx.astype(BF16), vc_ref[...])
    o_rows = [_dot(e_loc[t * GRID_W:(t + 1) * GRID_W, :], v_ref[pl.ds(starts[t], nloc), :]) for t in range(rb)]
    o_ref[...] = ((jnp.concatenate(o_rows, axis=0) + o_ctx) / den).astype(BF16)


def _nat(pb, pbc, tt, l):
    length = pb.shape[0]
    lc = pbc.shape[0]
    rows = length // GRID_W
    rb = min(8, rows)
    npair = 2 * NAT_KR - 2
    return pl.pallas_call(
        functools.partial(_nat_kernel, rb=rb, rows=rows),
        grid=(N_NAT, rows // rb),
        in_specs=[pl.BlockSpec((rb * GRID_W, LANES), lambda h, j: (j, PB_NQ // LANES + h)),
                  pl.BlockSpec((length, LANES), lambda h, j: (0, PB_NK // LANES + h)),
                  pl.BlockSpec((length, LANES), lambda h, j: (0, PB_NV // LANES + h)),
                  pl.BlockSpec((lc, LANES), lambda h, j: (0, PB_NK // LANES + h)),
                  pl.BlockSpec((lc, LANES), lambda h, j: (0, PB_NV // LANES + h)),
                  pl.BlockSpec((1, 1, npair, GRID_W, 2 * GRID_W), lambda h, j: (l, h, 0, 0, 0))],
        out_specs=pl.BlockSpec((rb * GRID_W, LANES), lambda h, j: (j, h)),
        out_shape=jax.ShapeDtypeStruct((length, D_NAT), BF16),
        compiler_params=_cparams(("arbitrary", "arbitrary")),
        name="nat",
    )(pb, pb, pb, pbc, pbc, tt)


def _ctx_attn_kernel(q_ref, k_ref, v_ref, o_ref):
    s = _dot_nt(q_ref[...], k_ref[...]) * (HEAD_DIM ** -0.5)
    e = jnp.exp(s - jnp.max(s, axis=-1, keepdims=True))
    den = jnp.sum(e, axis=-1, keepdims=True)
    o_ref[...] = (_dot(e.astype(BF16), v_ref[...]) / den).astype(BF16)


def _ctx_attn(pbc):
    lc = pbc.shape[0]
    return pl.pallas_call(
        _ctx_attn_kernel,
        grid=(N_NAT,),
        in_specs=[pl.BlockSpec((lc, LANES), lambda h: (0, PB_NQ // LANES + h)),
                  pl.BlockSpec((lc, LANES), lambda h: (0, PB_NK // LANES + h)),
                  pl.BlockSpec((lc, LANES), lambda h: (0, PB_NV // LANES + h))],
        out_specs=pl.BlockSpec((lc, LANES), lambda h: (0, h)),
        out_shape=jax.ShapeDtypeStruct((lc, D_NAT), BF16),
        compiler_params=_cparams(("arbitrary",)),
        name="ctx_attn",
    )(pbc, pbc, pbc)


def _mm_post_kernel(*refs, splits, ni, n_total, cast_w_in):
    na = len(splits)
    a_refs = refs[:na]
    if cast_w_in:
        w_ref, x_ref, gain_ref, gate_ref, win_ref, o_ref, wcat_ref, wa_ref, acc_ref, ss_ref = refs[na:]
        _cast_w_in(win_ref, wcat_ref, wa_ref)
    else:
        w_ref, x_ref, gain_ref, gate_ref, o_ref, acc_ref, ss_ref = refs[na:]
    i = pl.program_id(0)
    j = pl.program_id(1)
    slot = i % 2

    @pl.when(i < ni)
    def _():
        acc = None
        off = 0
        for a_ref, kk in zip(a_refs, splits):
            part = _dot(a_ref[...], w_ref[off:off + kk, :])
            acc = part if acc is None else acc + part
            off += kk
        acc_ref[slot, j] = acc
        sq = jnp.sum(acc * acc, axis=-1, keepdims=True)
        ss_ref[slot] = jnp.where(j == 0, sq, ss_ref[slot] + sq)

    @pl.when(i > 0)
    def _():
        prev = 1 - slot
        rstd = lax.rsqrt(ss_ref[prev] * (1.0 / n_total) + EPS)
        o_ref[...] = x_ref[...] + gate_ref[...] * (acc_ref[prev, j] * rstd * gain_ref[...])


W_IN_SIZES = (D_FOURIER, 2 * D_GLA_K, D_GLA, D_GLA, 2 * GLA_RANK, 3 * D_NAT)
W_IN_OFFS = tuple(int(t) for t in np.concatenate([[0], np.cumsum(W_IN_SIZES)]))


def _cast_w_in(s_ref, cat_ref, a_ref):
    o = W_IN_OFFS
    order = (0, 2, 5, 1, 3)
    dst = 0
    for t in order:
        wd = W_IN_SIZES[t]
        cat_ref[:, dst:dst + wd] = s_ref[0, :, o[t]:o[t] + wd].astype(BF16)
        dst += wd
    a = s_ref[0, :, o[4]:o[4] + LANES]
    lane = lax.broadcasted_iota(jnp.int32, a.shape, 1)
    a_ref[...] = jnp.where(lane < W_IN_SIZES[4], a, 0.0).astype(BF16)


def _mm_post(a_list, w, x, gain, gate, cast_w_in=None):
    m = x.shape[0]
    splits = tuple(a.shape[1] for a in a_list)
    k, n = w.shape
    bm = min(512, m)
    bn = 1024 if k <= 4096 else 512
    ni = m // bm
    nj = n // bn

    def prev_tile(i, j):
        return (jnp.maximum(i - 1, 0), jnp.where(i == 0, 0, j))

    in_specs = [pl.BlockSpec((bm, kk), lambda i, j: (jnp.minimum(i, ni - 1), 0)) for kk in splits]
    in_specs += [pl.BlockSpec((k, bn), lambda i, j: (0, jnp.where(i == ni, nj - 1, j))),
                 pl.BlockSpec((bm, bn), prev_tile),
                 pl.BlockSpec((1, bn), lambda i, j: (0, j)),
                 pl.BlockSpec((1, bn), lambda i, j: (0, j))]
    out_specs = [pl.BlockSpec((bm, bn), prev_tile)]
    out_shape = [jax.ShapeDtypeStruct((m, n), F32)]
    args = list(a_list) + [w, x, gain, gate]
    if cast_w_in is not None:
        c_in, c_out, c_shapes = _cast_specs([cast_w_in], [(N_PB + N_PF, LANES)], nj, (ni + 1) * nj)
        in_specs += c_in
        out_specs += c_out
        out_shape += c_shapes
        args.append(cast_w_in[0])
    res = pl.pallas_call(
        functools.partial(_mm_post_kernel, splits=splits, ni=ni, n_total=n, cast_w_in=cast_w_in is not None),
        grid=(ni + 1, nj),
        in_specs=in_specs,
        out_specs=out_specs,
        out_shape=out_shape,
        scratch_shapes=[pltpu.VMEM((2, nj, bm, bn), F32), pltpu.VMEM((2, bm, 1), F32)],
        compiler_params=_cparams(("arbitrary", "arbitrary")),
        name="mm_post",
    )(*args)
    return res if cast_w_in is not None else res[0]


FFN_HALO = BF16_SUBLANES // 2
FFN_BN = 512


def _ffn_up_kernel(*refs, bm, ni, ncast):
    x_ref, xp_ref, xn_ref, gain_ref, sc_ref, sh_ref, wg_ref, wv_ref, cg_ref, cv_ref = refs[:10]
    cast_src = refs[10:10 + ncast]
    o_ref = refs[10 + ncast]
    cast_dst = refs[11 + ncast:11 + 2 * ncast]
    h_ref = refs[11 + 2 * ncast]
    _cast_plain(cast_src, cast_dst)
    i = pl.program_id(0)
    j = pl.program_id(1)
    hl = FFN_HALO

    @pl.when(j == 0)
    def _():
        gain, sc, sh = gain_ref[...], sc_ref[...], sh_ref[...]
        h_ref[0:bm, :] = _modulate(x_ref[...], gain, sc, sh).astype(BF16)
        hp = _modulate(xp_ref[...], gain, sc, sh)
        hn = _modulate(xn_ref[...], gain, sc, sh)
        halo = jnp.concatenate([jnp.where(i > 0, hp, 0.0), jnp.where(i < ni - 1, hn, 0.0)], axis=0)
        h_ref[bm:bm + 2 * hl, :] = halo.astype(BF16)

    h = h_ref[...]
    row = lax.broadcasted_iota(jnp.int32, (bm, o_ref.shape[1]), 0)

    def conv(w_ref, cw_ref):
        y = _dot(h, w_ref[...])
        ym = y[0:bm, :]
        y_prev = jnp.where(row == 0, y[bm + hl - 1:bm + hl, :], pltpu.roll(ym, 1, 0))
        y_next = jnp.where(row == bm - 1, y[bm + hl:bm + hl + 1, :], pltpu.roll(ym, bm - 1, 0))
        return y_prev * cw_ref[0:1, :] + ym * cw_ref[1:2, :] + y_next * cw_ref[2:3, :]

    gate = conv(wg_ref, cg_ref)
    val = conv(wv_ref, cv_ref)
    o_ref[...] = (gate * jax.nn.sigmoid(gate) * val).astype(BF16)


def _ffn_up(x, gain, scale, shift, w, cw, cast=()):
    m, d = x.shape
    bm = min(512, m)
    ni = m // bm
    bn = FFN_BN
    nt = D_FF // bn
    hl = FFN_HALO
    per = bm // hl
    nhalo = m // hl
    vec = pl.BlockSpec((1, d), lambda i, j: (0, 0))
    c_in, c_out, c_shapes = _cast_specs(cast, [(s.shape[2],) for s, _ in cast], nt, ni * nt)
    res = pl.pallas_call(
        functools.partial(_ffn_up_kernel, bm=bm, ni=ni, ncast=len(cast)),
        grid=(ni, nt),
        in_specs=[pl.BlockSpec((bm, d), lambda i, j: (i, 0)),
                  pl.BlockSpec((hl, d), lambda i, j: (jnp.maximum(i * per - 1, 0), 0)),
                  pl.BlockSpec((hl, d), lambda i, j: (jnp.minimum((i + 1) * per, nhalo - 1), 0)),
                  vec, vec, vec,
                  pl.BlockSpec((d, bn), lambda i, j: (0, j)),
                  pl.BlockSpec((d, bn), lambda i, j: (0, nt + j)),
                  pl.BlockSpec((3, bn), lambda i, j: (0, j)),
                  pl.BlockSpec((3, bn), lambda i, j: (0, nt + j))] + c_in,
        out_specs=[pl.BlockSpec((bm, bn), lambda i, j: (i, j))] + c_out,
        out_shape=[jax.ShapeDtypeStruct((m, D_FF), BF16)] + c_shapes,
        scratch_shapes=[pltpu.VMEM((bm + 2 * hl, d), BF16)],
        compiler_params=_cparams(("arbitrary", "arbitrary")),
        name="ffn_up",
    )(x, x, x, gain, scale, shift, w, w, cw, cw, *[s for s, _ in cast])
    return res if cast else res[0]


def _prep_w_in(w):
    f, qk, v, g, a, nat = [w[:, W_IN_OFFS[t]:W_IN_OFFS[t + 1]] for t in range(6)]
    w_cat = jnp.concatenate([f, v, nat, qk, g], axis=1).astype(BF16)
    w_a = jnp.pad(a, ((0, 0), (0, LANES - 2 * GLA_RANK))).astype(BF16)
    return w_cat, w_a


def _prep_decay(w_dec, b_dec):
    wd = jnp.zeros((2, LANES, D_GLA_K), F32)
    wd = wd.at[0, 0:GLA_RANK].set(w_dec[0]).at[1, GLA_RANK:2 * GLA_RANK].set(w_dec[1])
    return wd, b_dec.reshape(2, 1, D_GLA_K)


def _state_zero():
    return jnp.zeros((N_GLA // 2, 2 * HEAD_DIM, LANES), F32)


def kernel(x, c, ctx, c_ctx, ada_w, ada_b, norm_mix_pre, norm_mix_post, w_in, gla_w_decay, gla_b_decay,
           gla_norm, fourier_w, nat_rpb, w_out, norm_ffn_pre, norm_ffn_post, ffn_w_up, ffn_w_conv, ffn_w_down):
    depth = ada_w.shape[0]
    d = x.shape[-1]
    xs = x[0]
    cs = ctx[0]
    seq = xs.shape[0]
    lc = cs.shape[0]

    mod = _ada_mod(c, c_ctx, ada_w, ada_b)
    cos_x, sin_x = _rope_tables(seq, True)
    cos_c, sin_c = _rope_tables(lc, False)

    w_cat, w_a = _prep_w_in(w_in[0])
    w_out_b = w_out[0].astype(BF16)
    w_up_b = ffn_w_up[0].astype(BF16)
    w_down_b = ffn_w_down[0].astype(BF16)
    tt = _nat_bias_table(nat_rpb)

    for l in range(depth):
        last = l == depth - 1
        mx = [mod[l, 0:1, t * d:(t + 1) * d] for t in range(N_MOD)]
        mc = [mod[l, 1:2, t * d:(t + 1) * d] for t in range(N_MOD)]
        vrow = lambda t: t[l].reshape(1, -1)

        wd2, bd2 = _prep_decay(gla_w_decay[l], gla_b_decay[l])
        wc = _four_weights(fourier_w[l])
        gn = vrow(gla_norm)

        res = _inproj(xs, vrow(norm_mix_pre), mx[1], mx[0], w_cat, w_a,
                      cast=() if last else ((ffn_w_up, l + 1),))
        pbx, pfx, pax = res[:3]
        pbc, pfc, pac = _inproj(cs, vrow(norm_mix_pre), mc[1], mc[0], w_cat, w_a)

        gla_c, sc_f, sc_b = _gla_bidir(pfc, pac, pbc, cos_c, sin_c, wd2, bd2, gn, _state_zero(), _state_zero())
        gla_x, _, _ = _gla_bidir(pfx, pax, pbx, cos_x, sin_x, wd2, bd2, gn, sc_f, sc_b)
        four_x = _fourier_mix(pbx, wc)
        nat_x = _nat(pbx, pbc, tt, l)
        xs = _mm_post([four_x, gla_x, nat_x], w_out_b, xs, vrow(norm_mix_post), mx[2])
        if not last:
            four_c = _fourier_mix(pbc, wc)
            nat_c = _ctx_attn(pbc)
            cs = _mm_post([four_c, gla_c, nat_c], w_out_b, cs, vrow(norm_mix_post), mc[2])

        if last:
            act_x = _ffn_up(xs, vrow(norm_ffn_pre), mx[4], mx[3], w_up_b, ffn_w_conv[l])
            xs = _mm_post([act_x], w_down_b, xs, vrow(norm_ffn_post), mx[5])
        else:
            act_x, w_down_n, w_out_n = _ffn_up(xs, vrow(norm_ffn_pre), mx[4], mx[3], w_up_b, ffn_w_conv[l],
                                               cast=((ffn_w_down, l + 1), (w_out, l + 1)))
            xs, w_cat_n, w_a_n = _mm_post([act_x], w_down_b, xs, vrow(norm_ffn_post), mx[5],
                                          cast_w_in=(w_in, l + 1))
            act_c = _ffn_up(cs, vrow(norm_ffn_pre), mc[4], mc[3], w_up_b, ffn_w_conv[l])
            cs = _mm_post([act_c], w_down_b, cs, vrow(norm_ffn_post), mc[5])
            w_cat, w_a, w_out_b, w_up_b, w_down_b = w_cat_n, w_a_n, w_out_n, res[3], w_down_n
    return xs[None]
```

```python
import functools

import numpy as np
import jax
import jax.numpy as jnp
from jax import lax
from jax.experimental import pallas as pl
from jax.experimental.pallas import tpu as pltpu

F32 = jnp.float32
BF16 = jnp.bfloat16

D_MODEL = 4096
GRID_W = 64
HEAD_DIM = 128
N_FOURIER = 8
N_GLA = 12
N_NAT = 12
D_FOURIER = N_FOURIER * HEAD_DIM
D_GLA = N_GLA * HEAD_DIM
GLA_DK = HEAD_DIM // 2
D_GLA_K = N_GLA * GLA_DK
GLA_RANK = 16
GLA_TAU = 16.0
GLA_CHUNK = 64
D_NAT = N_NAT * HEAD_DIM
NAT_KR = 8
NAT_KC = 16
ROPE_BASE = 10000.0
D_FF = 11 * D_MODEL // 8
N_MOD = 6
EPS = 1e-6
NEG = -1e30

LANES = 128
BF16_SUBLANES = 16
VMEM_LIMIT_MB = 56

PB_F, PB_V, PB_NQ, PB_NK, PB_NV = 0, 1024, 2560, 4096, 5632
N_PB = 7168
PF_Q, PF_K, PF_G = 0, 768, 1536
N_PF = 3072
IN_BN = 1024
GLA_SLABS = 2


def _cparams(sem):
    return pltpu.CompilerParams(dimension_semantics=sem, vmem_limit_bytes=VMEM_LIMIT_MB * 1024 * 1024)


def _dot(a, b):
    return jnp.dot(a, b, preferred_element_type=F32)


def _dot_nt(a, b):
    return lax.dot_general(a, b, (((1,), (1,)), ((), ())), preferred_element_type=F32)


def _dot_tn(a, b):
    return lax.dot_general(a, b, (((0,), (0,)), ((), ())), preferred_element_type=F32)


def _split_bf16(x):
    hi = x.astype(BF16)
    lo = (x - hi.astype(F32)).astype(BF16)
    return hi, lo


def _ada_kernel(cc_ref, w_ref, b_ref, o_ref):
    s = cc_ref[...]
    s = s * jax.nn.sigmoid(s)
    s_hi = s.astype(BF16).astype(F32)
    row = lax.broadcasted_iota(jnp.int32, s.shape, 0)
    lhs = jnp.where(row < 8, s_hi, s - s_hi).astype(BF16)
    w_hi, w_lo = _split_bf16(w_ref[0])
    r = _dot(lhs, w_hi) + _dot(lhs, w_lo)
    o_ref[0] = r[0:8] + r[8:16] + b_ref[0]


def _ada_mod(c, c_ctx, ada_w, ada_b):
    depth, d, n = ada_w.shape
    bn = 512
    cc = jnp.zeros((16, d), F32)
    cc = cc.at[0].set(c[0]).at[1].set(c_ctx).at[8].set(c[0]).at[9].set(c_ctx)
    return pl.pallas_call(
        _ada_kernel,
        grid=(depth, n // bn),
        in_specs=[pl.BlockSpec((16, d), lambda l, j: (0, 0)),
                  pl.BlockSpec((1, d, bn), lambda l, j: (l, 0, j)),
                  pl.BlockSpec((1, 1, bn), lambda l, j: (l, 0, j))],
        out_specs=pl.BlockSpec((1, 8, bn), lambda l, j: (l, 0, j)),
        out_shape=jax.ShapeDtypeStruct((depth, 8, n), F32),
        compiler_params=_cparams(("arbitrary", "arbitrary")),
        name="ada_mod",
    )(cc, ada_w, ada_b.reshape(depth, 1, n))


def _modulate(x, gain, scale, shift):
    ms = jnp.mean(x * x, axis=-1, keepdims=True)
    y = x * lax.rsqrt(ms + EPS) * gain
    return y * (1.0 + scale) + shift


CAST_ROWS = BF16_SUBLANES


def _cast_specs(srcs, out_widths, nj, nsteps):
    in_specs, out_specs, out_shapes = [], [], []
    for (s, l), widths in zip(srcs, out_widths):
        nrow, ncol = s.shape[1], s.shape[2]
        rows = next(r for r in range(CAST_ROWS, nrow + 1, CAST_ROWS) if nrow % r == 0 and nrow // r <= nsteps)
        nchunk = nrow // rows

        def chunk(i, j, nchunk=nchunk):
            return jnp.minimum(i * nj + j, nchunk - 1)

        in_specs.append(pl.BlockSpec((1, rows, ncol), lambda i, j, l=l, chunk=chunk: (l, chunk(i, j), 0)))
        for wd in widths:
            out_specs.append(pl.BlockSpec((rows, wd), lambda i, j, chunk=chunk: (chunk(i, j), 0)))
            out_shapes.append(jax.ShapeDtypeStruct((nrow, wd), BF16))
    return in_specs, out_specs, out_shapes


def _cast_plain(src_refs, dst_refs):
    for s_ref, d_ref in zip(src_refs, dst_refs):
        d_ref[...] = s_ref[0].astype(BF16)


W_IN_SIZES = (D_FOURIER, 2 * D_GLA_K, D_GLA, D_GLA, 2 * GLA_RANK, 3 * D_NAT)
W_IN_OFFS = tuple(int(t) for t in np.concatenate([[0], np.cumsum(W_IN_SIZES)]))
W_IN_ORDER = (0, 2, 5, 1, 3)
W_IN_CAST_ROWS = 32


def _w_in_src_block(c):
    r = W_IN_CAST_ROWS
    src = None
    dst_lo = 0
    for t in W_IN_ORDER:
        n = W_IN_SIZES[t] // r
        here = W_IN_OFFS[t] // r + (c - dst_lo)
        src = here if src is None else jnp.where(c >= dst_lo, here, src)
        dst_lo += n
    return src


def _inproj_kernel(*refs, nb_tiles, cast):
    if cast:
        x_ref, gain_ref, sc_ref, sh_ref, w_ref, wa_ref, src_ref, ob_ref, of_ref, oa_ref, dst_ref, h_ref = refs
        dst_ref[...] = src_ref[0].astype(BF16)
    else:
        x_ref, gain_ref, sc_ref, sh_ref, w_ref, wa_ref, ob_ref, of_ref, oa_ref, h_ref = refs
    j = pl.program_id(1)

    @pl.when(j == 0)
    def _():
        h = _modulate(x_ref[...], gain_ref[...], sc_ref[...], sh_ref[...]).astype(BF16)
        h_ref[...] = h
        oa_ref[...] = _dot_nt(h, wa_ref[0])

    r = _dot_nt(h_ref[...], w_ref[...])

    @pl.when(j < nb_tiles)
    def _():
        ob_ref[...] = r.astype(BF16)

    @pl.when(j >= nb_tiles)
    def _():
        of_ref[...] = r


def _inproj(x, gain, scale, shift, wt, wat, l, cast_src=None):
    m, d = x.shape
    bm = min(512, m)
    bn = IN_BN
    nb_tiles = N_PB // bn
    nf_tiles = N_PF // bn
    nj = nb_tiles + nf_tiles
    nrow = N_PB + N_PF
    vec = pl.BlockSpec((1, d), lambda i, j: (0, 0))
    in_specs = [pl.BlockSpec((bm, d), lambda i, j: (i, 0)), vec, vec, vec,
                pl.BlockSpec((bn, d), lambda i, j: (j, 0)),
                pl.BlockSpec((1, LANES, d), lambda i, j: (l, 0, 0))]
    out_specs = [pl.BlockSpec((bm, bn), lambda i, j: (i, jnp.minimum(j, nb_tiles - 1))),
                 pl.BlockSpec((bm, bn), lambda i, j: (i, jnp.maximum(j - nb_tiles, 0))),
                 pl.BlockSpec((bm, LANES), lambda i, j: (i, 0))]
    out_shape = [jax.ShapeDtypeStruct((m, N_PB), BF16), jax.ShapeDtypeStruct((m, N_PF), F32),
                 jax.ShapeDtypeStruct((m, LANES), F32)]
    args = [x, gain, scale, shift, wt, wat]
    if cast_src is not None:
        src, ls = cast_src
        nchunk = nrow // W_IN_CAST_ROWS
        assert nchunk <= (m // bm) * nj

        def chunk(i, j):
            return jnp.minimum(i * nj + j, nchunk - 1)

        in_specs.append(pl.BlockSpec((1, W_IN_CAST_ROWS, d), lambda i, j: (ls, _w_in_src_block(chunk(i, j)), 0)))
        out_specs.append(pl.BlockSpec((W_IN_CAST_ROWS, d), lambda i, j: (chunk(i, j), 0)))
        out_shape.append(jax.ShapeDtypeStruct((nrow, d), BF16))
        args.append(src)
    return pl.pallas_call(
        functools.partial(_inproj_kernel, nb_tiles=nb_tiles, cast=cast_src is not None),
        grid=(m // bm, nj),
        in_specs=in_specs,
        out_specs=out_specs,
        out_shape=out_shape,
        scratch_shapes=[pltpu.VMEM((bm, d), BF16)],
        compiler_params=_cparams(("arbitrary", "arbitrary")),
        name="inproj",
    )(*args)


def _four_w_kernel(cc_ref, sc_ref, fw_ref, o_ref):
    c_hi, c_lo = _split_bf16(cc_ref[...])
    s_hi, s_lo = _split_bf16(sc_ref[...])
    for g in range(N_FOURIER):
        w_hi, w_lo = _split_bf16(fw_ref[g])
        a = _dot(c_hi, w_hi) + _dot(c_hi, w_lo) + _dot(c_lo, w_hi)
        b = _dot(s_hi, w_hi) + _dot(s_hi, w_lo) + _dot(s_lo, w_hi)
        o_ref[g, :, 0:HEAD_DIM] = a
        o_ref[g, :, HEAD_DIM:2 * HEAD_DIM] = -b


def _four_weights(fourier_w_l):
    k = np.arange(HEAD_DIM)
    ang = 2.0 * np.pi * ((k[:, None] * k[None, :]) % HEAD_DIM) / HEAD_DIM
    cc = jnp.asarray(np.cos(ang), F32)
    sc = jnp.asarray(np.sin(ang), F32)
    return pl.pallas_call(
        _four_w_kernel,
        out_shape=jax.ShapeDtypeStruct((N_FOURIER, HEAD_DIM, 2 * HEAD_DIM), F32),
        name="four_w",
    )(cc, sc, fourier_w_l)


def _four_s0_kernel(u_ref, wc_ref, v_ref, *, norm):
    for g in range(N_FOURIER):
        cols = slice(g * HEAD_DIM, (g + 1) * HEAD_DIM)
        y = _dot(u_ref[:, cols], wc_ref[g].astype(BF16)) * norm
        v_ref[0, :, cols] = y[:, 0:HEAD_DIM].astype(BF16)
        v_ref[1, :, cols] = y[:, HEAD_DIM:2 * HEAD_DIM].astype(BF16)


def _four_s1_kernel(m1_ref, v_ref, z_ref):
    z_ref[...] = _dot(m1_ref[...].astype(BF16), v_ref[...]).astype(BF16)


def _four_s2_kernel(t_ref, z_ref, o_ref):
    kb, n2, ch = z_ref.shape[1], z_ref.shape[2], z_ref.shape[3]
    for kk in range(kb):
        z = jnp.concatenate([z_ref[0, kk], z_ref[1, kk]], axis=0)
        o_ref[:, kk * ch:(kk + 1) * ch] = _dot(t_ref[kk].astype(BF16), z).astype(BF16)


@functools.lru_cache(maxsize=None)
def _four_tables(n1, n2):
    length = n1 * n2
    m1 = None
    if n1 > 1:
        j = np.arange(n1)
        ang = 2.0 * np.pi * ((j[:, None] * j[None, :]) % n1) / n1
        cs, ss = np.cos(ang), np.sin(ang)
        m1 = np.block([[cs, ss], [-ss, cs]]).astype(np.float32)
    k1 = np.arange(n1, dtype=np.int64)[:, None, None]
    k2 = np.arange(n2, dtype=np.int64)[None, :, None]
    j2 = np.arange(n2, dtype=np.int64)[None, None, :]
    ang = 2.0 * np.pi * ((j2 * (k1 + n1 * k2)) % length) / length
    t2 = np.concatenate([np.cos(ang), np.sin(ang)], axis=-1).astype(np.float32)
    return m1, t2


def _fourier_mix(pb, wc):
    length = pb.shape[0]
    ch = D_FOURIER
    n2 = length if length <= 512 else 128
    n1 = length // n2
    m1, t2 = _four_tables(n1, n2)
    norm = float(1.0 / np.sqrt(float(length) * HEAD_DIM))
    bm = min(512, length)
    v = pl.pallas_call(
        functools.partial(_four_s0_kernel, norm=norm),
        grid=(length // bm,),
        in_specs=[pl.BlockSpec((bm, ch), lambda i: (i, 0)),
                  pl.BlockSpec((N_FOURIER, HEAD_DIM, 2 * HEAD_DIM), lambda i: (0, 0, 0))],
        out_specs=pl.BlockSpec((2, bm, ch), lambda i: (0, i, 0)),
        out_shape=jax.ShapeDtypeStruct((2, length, ch), BF16),
        compiler_params=_cparams(("arbitrary",)),
        name="four_s0",
    )(pb, wc)
    if n1 > 1:
        ncol = n2 * ch
        bc = min(4096, ncol)
        z = pl.pallas_call(
            _four_s1_kernel,
            grid=(ncol // bc,),
            in_specs=[pl.BlockSpec((2 * n1, 2 * n1), lambda i: (0, 0)),
                      pl.BlockSpec((2 * n1, bc), lambda i: (0, i))],
            out_specs=pl.BlockSpec((2 * n1, bc), lambda i: (0, i)),
            out_shape=jax.ShapeDtypeStruct((2 * n1, ncol), BF16),
            compiler_params=_cparams(("arbitrary",)),
            name="four_s1",
        )(jnp.asarray(m1), v.reshape(2 * n1, ncol))
    else:
        z = v
    z4 = z.reshape(2, n1, n2, ch)
    kb = min(4, n1)
    out = pl.pallas_call(
        _four_s2_kernel,
        grid=(n1 // kb,),
        in_specs=[pl.BlockSpec((kb, n2, 2 * n2), lambda i: (i, 0, 0)),
                  pl.BlockSpec((2, kb, n2, ch), lambda i: (0, i, 0, 0))],
        out_specs=pl.BlockSpec((n2, kb * ch), lambda i: (0, i)),
        out_shape=jax.ShapeDtypeStruct((n2, n1 * ch), BF16),
        compiler_params=_cparams(("arbitrary",)),
        name="four_s2",
    )(jnp.asarray(t2), z4)
    return out.reshape(length, ch)


def _rope_tables(length, use_rope):
    if not use_rope:
        return jnp.ones((length, LANES), F32), jnp.zeros((length, LANES), F32)
    pos = jnp.arange(length)
    prow = (pos // GRID_W).astype(F32)
    pcol = (pos % GRID_W).astype(F32)
    half = GLA_DK // 4
    freqs = ROPE_BASE ** (-jnp.arange(half, dtype=F32) / half)
    ar = prow[:, None] * freqs
    ac = pcol[:, None] * freqs
    cos64 = jnp.concatenate([jnp.cos(ar), jnp.cos(ar), jnp.cos(ac), jnp.cos(ac)], axis=-1)
    sin64 = jnp.concatenate([-jnp.sin(ar), jnp.sin(ar), -jnp.sin(ac), jnp.sin(ac)], axis=-1)
    return jnp.concatenate([cos64, cos64], axis=-1), jnp.concatenate([sin64, sin64], axis=-1)


def _rope(x, cos, sin):
    lane = lax.broadcasted_iota(jnp.int32, x.shape, 1)
    first = (lane % 32) < 16
    partner = jnp.where(first, pltpu.roll(x, LANES - 16, 1), pltpu.roll(x, 16, 1))
    return x * cos + partner * sin


def _gla_kernel(*refs, reverse, nchunk, nblk, finalize):
    if finalize:
        (q_ref, k_ref, a_ref, v_ref, cos_ref, sin_ref, wd_ref, bd_ref, tri_ref, s0_ref,
         of_ref, g_ref, gn_ref, o_ref, sfin_ref, st_ref) = refs
    else:
        (q_ref, k_ref, a_ref, v_ref, cos_ref, sin_ref, wd_ref, bd_ref, tri_ref, s0_ref,
         o_ref, sfin_ref, st_ref) = refs
        of_ref = g_ref = gn_ref = None
    jb = pl.program_id(1)

    @pl.when(jb == 0)
    def _():
        st_ref[...] = s0_ref[...]

    finals = []
    for sl in range(GLA_SLABS):
        one = slice(sl * LANES, (sl + 1) * LANES)
        two = slice(2 * sl * LANES, 2 * (sl + 1) * LANES)
        finals.append(_gla_slab(
            q_ref[:, one], k_ref[:, one], a_ref[...], v_ref, two, cos_ref[...], sin_ref[...],
            wd_ref[:, one], bd_ref[:, one], tri_ref[...], st_ref[sl],
            of_ref, g_ref, gn_ref, o_ref, reverse=reverse, nchunk=nchunk, finalize=finalize))
    for sl in range(GLA_SLABS):
        st_ref[sl] = finals[sl]

    @pl.when(jb == nblk - 1)
    def _():
        for sl in range(GLA_SLABS):
            sfin_ref[sl] = finals[sl]


def _gla_slab(q, k, a, v_ref, two, cos, sin, wd, bd, tri, s, of_ref, g_ref, gn_ref, o_ref, *,
              reverse, nchunk, finalize):
    c = GLA_CHUNK
    z = _dot(a.astype(BF16), wd.astype(BF16)) + bd
    g = (jnp.minimum(z, 0.0) - jnp.log(1.0 + jnp.exp(-jnp.abs(z)))) * (1.0 / GLA_TAU)
    g_hi, g_lo = _split_bf16(g)
    bsum = _dot(tri, jnp.concatenate([g_hi, g_lo], axis=1))
    b = bsum[:, 0:LANES] + bsum[:, LANES:2 * LANES]
    q_dec = _rope(q, cos, sin) * (GLA_DK ** -0.5) * jnp.exp(b)
    k_inv = _rope(k, cos, sin) * jnp.exp(-b)

    lane = lax.broadcasted_iota(jnp.int32, (c, LANES), 1)
    row = lax.broadcasted_iota(jnp.int32, (c, LANES), 0)
    head0 = lane < GLA_DK
    key_pos = lane % c
    causal = (key_pos >= row) if reverse else (key_pos <= row)
    lane2 = lax.broadcasted_iota(jnp.int32, (c, 2 * LANES), 1)
    vhead0 = lane2 < LANES
    srow = lax.broadcasted_iota(jnp.int32, (2 * HEAD_DIM, LANES), 0)
    slane = lax.broadcasted_iota(jnp.int32, (2 * HEAD_DIM, LANES), 1)
    state_mask = (srow < HEAD_DIM) == (slane < GLA_DK)

    order = range(nchunk - 1, -1, -1) if reverse else range(nchunk)
    for ci in order:
        lo = ci * c
        last = lo if reverse else lo + c - 1
        dec = jnp.exp(b[last:last + 1, :])
        qd = q_dec[lo:lo + c, :].astype(BF16)
        ki = k_inv[lo:lo + c, :]
        ke = (ki * dec).astype(BF16)
        zero = jnp.zeros_like(ki)
        kbd = jnp.concatenate([jnp.where(head0, ki, zero), jnp.where(head0, zero, ki)], axis=0).astype(BF16)
        att = _dot_nt(qd, kbd)
        att = jnp.where(causal, att, 0.0).astype(BF16)
        vc = v_ref[lo:lo + c, two]
        vzero = jnp.zeros_like(vc)
        vbd = jnp.concatenate([jnp.where(vhead0, vc, vzero), jnp.where(vhead0, vzero, vc)], axis=0)
        o = _dot(att, vbd) + _dot_nt(qd, s.astype(BF16))
        if finalize:
            o = o + of_ref[lo:lo + c, two]
            gate = g_ref[lo:lo + c, two]
            gate = gate * jax.nn.sigmoid(gate)
            gn = gn_ref[...]
            outs = []
            for h in range(2):
                oh = o[:, h * HEAD_DIM:(h + 1) * HEAD_DIM]
                ms = jnp.mean(oh * oh, axis=-1, keepdims=True)
                outs.append(oh * lax.rsqrt(ms + EPS) * gn)
            o_ref[lo:lo + c, two] = (jnp.concatenate(outs, axis=1) * gate).astype(o_ref.dtype)
        else:
            o_ref[lo:lo + c, two] = o
        ut = _dot_tn(vc, ke)
        s = dec * s + jnp.where(state_mask, ut, 0.0)
    return s


def _gla_tri(tb, reverse):
    t = np.arange(tb)
    same = (t[:, None] // GLA_CHUNK) == (t[None, :] // GLA_CHUNK)
    tri = (t[None, :] >= t[:, None]) if reverse else (t[None, :] <= t[:, None])
    return jnp.asarray((same & tri).astype(np.float32), BF16)


def _gla_pass(pf, pa, pb, cos, sin, wd, bd, s0, reverse, fin=None):
    m = pf.shape[0]
    tb = min(512, m)
    nblk = m // tb
    npair = N_GLA // 2
    finalize = fin is not None

    def rowblk(j):
        return (nblk - 1 - j) if reverse else j

    ns = GLA_SLABS
    w1 = ns * LANES
    w2 = 2 * w1
    in_specs = [
        pl.BlockSpec((tb, w1), lambda h, j: (rowblk(j), PF_Q // w1 + h)),
        pl.BlockSpec((tb, w1), lambda h, j: (rowblk(j), PF_K // w1 + h)),
        pl.BlockSpec((tb, LANES), lambda h, j: (rowblk(j), 0)),
        pl.BlockSpec((tb, w2), lambda h, j: (rowblk(j), PB_V // w2 + h)),
        pl.BlockSpec((tb, LANES), lambda h, j: (rowblk(j), 0)),
        pl.BlockSpec((tb, LANES), lambda h, j: (rowblk(j), 0)),
        pl.BlockSpec((LANES, w1), lambda h, j: (0, h)),
        pl.BlockSpec((1, w1), lambda h, j: (0, h)),
        pl.BlockSpec((tb, tb), lambda h, j: (0, 0)),
        pl.BlockSpec((ns, 2 * HEAD_DIM, LANES), lambda h, j: (h, 0, 0)),
    ]
    args = [pf, pf, pa, pb, cos, sin, wd, bd, _gla_tri(tb, reverse), s0]
    if finalize:
        o_other, gn = fin
        in_specs += [
            pl.BlockSpec((tb, w2), lambda h, j: (rowblk(j), h)),
            pl.BlockSpec((tb, w2), lambda h, j: (rowblk(j), PF_G // w2 + h)),
            pl.BlockSpec((1, HEAD_DIM), lambda h, j: (0, 0)),
        ]
        args += [o_other, pf, gn]
    out_dtype = BF16 if finalize else F32
    return pl.pallas_call(
        functools.partial(_gla_kernel, reverse=reverse, nchunk=tb // GLA_CHUNK, nblk=nblk, finalize=finalize),
        grid=(npair // ns, nblk),
        in_specs=in_specs,
        out_specs=[pl.BlockSpec((tb, w2), lambda h, j: (rowblk(j), h)),
                   pl.BlockSpec((ns, 2 * HEAD_DIM, LANES), lambda h, j: (h, 0, 0))],
        out_shape=[jax.ShapeDtypeStruct((m, D_GLA), out_dtype),
                   jax.ShapeDtypeStruct((npair, 2 * HEAD_DIM, LANES), F32)],
        scratch_shapes=[pltpu.VMEM((ns, 2 * HEAD_DIM, LANES), F32)],
        compiler_params=_cparams(("arbitrary", "arbitrary")),
        name="gla_bwd" if reverse else "gla_fwd",
    )(*args)


def _gla_bidir(pf, pa, pb, cos, sin, wd2, bd2, gn, s0_f, s0_b):
    o_f, s_f = _gla_pass(pf, pa, pb, cos, sin, wd2[0], bd2[0], s0_f, reverse=False)
    out, s_b = _gla_pass(pf, pa, pb, cos, sin, wd2[1], bd2[1], s0_b, reverse=True, fin=(o_f, gn))
    return out, s_f, s_b


@functools.lru_cache(maxsize=None)
def _nat_table_consts():
    qcol = np.arange(GRID_W)
    cstart = np.clip(qcol - NAT_KC // 2, 0, GRID_W - NAT_KC)
    in_win = (qcol[None, :] >= cstart[:, None]) & (qcol[None, :] < cstart[:, None] + NAT_KC)
    col_idx = np.clip(qcol[None, :] - qcol[:, None] + NAT_KC - 1, 0, 2 * NAT_KC - 2)
    width = 2 * NAT_KC
    qc, half, kc = np.meshgrid(qcol, np.arange(2), qcol, indexing="ij")
    pos = (qc * 2 * GRID_W + half * GRID_W + kc).ravel()
    src = (half * width + col_idx[qc, kc]).ravel()
    valid = in_win[qc, kc].ravel()
    expand = np.zeros((2 * width, 2 * GRID_W * GRID_W), np.float32)
    expand[src[valid], pos[valid]] = 1.0
    neg = np.zeros((1, 2 * GRID_W * GRID_W), np.float32)
    neg[0, pos[~valid]] = NEG
    return expand, neg


def _nat_table_kernel(r_ref, e_ref, n_ref, o_ref):
    r = r_ref[...]
    hi = r.astype(BF16)
    rem = r - hi.astype(F32)
    mid = rem.astype(BF16)
    lo = (rem - mid.astype(F32)).astype(BF16)
    e = e_ref[...].astype(BF16)
    o_ref[...] = _dot(hi, e) + _dot(mid, e) + _dot(lo, e) + n_ref[...]


def _nat_bias_table(rpb_all):
    depth, nh = rpb_all.shape[0], rpb_all.shape[1]
    npair = 2 * NAT_KR - 2
    expand, neg = _nat_table_consts()
    rp = jnp.pad(rpb_all.astype(F32), ((0, 0), (0, 0), (0, 0), (0, 1)))
    rows = jnp.concatenate([rp[:, :, 0:npair], rp[:, :, 1:npair + 1]], axis=-1).reshape(depth * nh * npair, -1)
    ncol = expand.shape[1]
    bc = 2048
    out = pl.pallas_call(
        _nat_table_kernel,
        grid=(ncol // bc,),
        in_specs=[pl.BlockSpec(rows.shape, lambda i: (0, 0)),
                  pl.BlockSpec((expand.shape[0], bc), lambda i: (0, i)),
                  pl.BlockSpec((1, bc), lambda i: (0, i))],
        out_specs=pl.BlockSpec((rows.shape[0], bc), lambda i: (0, i)),
        out_shape=jax.ShapeDtypeStruct((rows.shape[0], ncol), F32),
        compiler_params=_cparams(("arbitrary",)),
        name="nat_table",
    )(rows, jnp.asarray(expand), jnp.asarray(neg))
    return out.reshape(depth, nh, npair, GRID_W, 2 * GRID_W)


def _nat_kernel(q_ref, k_ref, v_ref, kc_ref, vc_ref, tt_ref, o_ref, *, rb, rows):
    jb = pl.program_id(1)
    scale = HEAD_DIM ** -0.5
    nloc = NAT_KR * GRID_W
    q_all = q_ref[...]
    s_ctx = _dot_nt(q_all, kc_ref[...]) * scale
    starts = []
    s_rows = []
    for t in range(rb):
        r = jb * rb + t
        rs = jnp.clip(r - NAT_KR // 2, 0, rows - NAT_KR)
        d = r - rs
        start = pl.multiple_of(rs * GRID_W, GRID_W)
        starts.append(start)
        kw = k_ref[pl.ds(start, nloc), :]
        bias = jnp.concatenate([tt_ref[0, 0, NAT_KR - 1 - d + 2 * p] for p in range(NAT_KR // 2)], axis=1)
        s_rows.append(_dot_nt(q_all[t * GRID_W:(t + 1) * GRID_W, :], kw) * scale + bias)
    s_loc = jnp.concatenate(s_rows, axis=0)
    mx = jnp.maximum(jnp.max(s_loc, axis=-1, keepdims=True), jnp.max(s_ctx, axis=-1, keepdims=True))
    e_loc = jnp.exp(s_loc - mx)
    e_ctx = jnp.exp(s_ctx - mx)
    den = jnp.sum(e_loc, axis=-1, keepdims=True) + jnp.sum(e_ctx, axis=-1, keepdims=True)
    e_loc = e_loc.astype(BF16)
    o_ctx = _dot(e_ctx.astype(BF16), vc_ref[...])
    o_rows = [_dot(e_loc[t * GRID_W:(t + 1) * GRID_W, :], v_ref[pl.ds(starts[t], nloc), :]) for t in range(rb)]
    o_ref[...] = ((jnp.concatenate(o_rows, axis=0) + o_ctx) / den).astype(BF16)


def _nat(pb, pbc, tt, l):
    length = pb.shape[0]
    lc = pbc.shape[0]
    rows = length // GRID_W
    rb = min(8, rows)
    npair = 2 * NAT_KR - 2
    return pl.pallas_call(
        functools.partial(_nat_kernel, rb=rb, rows=rows),
        grid=(N_NAT, rows // rb),
        in_specs=[pl.BlockSpec((rb * GRID_W, LANES), lambda h, j: (j, PB_NQ // LANES + h)),
                  pl.BlockSpec((length, LANES), lambda h, j: (0, PB_NK // LANES + h)),
                  pl.BlockSpec((length, LANES), lambda h, j: (0, PB_NV // LANES + h)),
                  pl.BlockSpec((lc, LANES), lambda h, j: (0, PB_NK // LANES + h)),
                  pl.BlockSpec((lc, LANES), lambda h, j: (0, PB_NV // LANES + h)),
                  pl.BlockSpec((1, 1, npair, GRID_W, 2 * GRID_W), lambda h, j: (l, h, 0, 0, 0))],
        out_specs=pl.BlockSpec((rb * GRID_W, LANES), lambda h, j: (j, h)),
        out_shape=jax.ShapeDtypeStruct((length, D_NAT), BF16),
        compiler_params=_cparams(("arbitrary", "arbitrary")),
        name="nat",
    )(pb, pb, pb, pbc, pbc, tt)


def _ctx_attn_kernel(q_ref, k_ref, v_ref, o_ref):
    s = _dot_nt(q_ref[...], k_ref[...]) * (HEAD_DIM ** -0.5)
    e = jnp.exp(s - jnp.max(s, axis=-1, keepdims=True))
    den = jnp.sum(e, axis=-1, keepdims=True)
    o_ref[...] = (_dot(e.astype(BF16), v_ref[...]) / den).astype(BF16)


def _ctx_attn(pbc):
    lc = pbc.shape[0]
    return pl.pallas_call(
        _ctx_attn_kernel,
        grid=(N_NAT,),
        in_specs=[pl.BlockSpec((lc, LANES), lambda h: (0, PB_NQ // LANES + h)),
                  pl.BlockSpec((lc, LANES), lambda h: (0, PB_NK // LANES + h)),
                  pl.BlockSpec((lc, LANES), lambda h: (0, PB_NV // LANES + h))],
        out_specs=pl.BlockSpec((lc, LANES), lambda h: (0, h)),
        out_shape=jax.ShapeDtypeStruct((lc, D_NAT), BF16),
        compiler_params=_cparams(("arbitrary",)),
        name="ctx_attn",
    )(pbc, pbc, pbc)


def _mm_post_kernel(*refs, splits, ni, n_total, ncast):
    na = len(splits)
    a_refs = refs[:na]
    w_ref, x_ref, gain_ref, gate_ref = refs[na:na + 4]
    cast_src = refs[na + 4:na + 4 + ncast]
    o_ref = refs[na + 4 + ncast]
    cast_dst = refs[na + 5 + ncast:na + 5 + 2 * ncast]
    acc_ref, ss_ref = refs[na + 5 + 2 * ncast:]
    _cast_plain(cast_src, cast_dst)
    i = pl.program_id(0)
    j = pl.program_id(1)
    slot = i % 2

    @pl.when(i < ni)
    def _():
        acc = None
        off = 0
        for a_ref, kk in zip(a_refs, splits):
            part = _dot(a_ref[...], w_ref[off:off + kk, :])
            acc = part if acc is None else acc + part
            off += kk
        acc_ref[slot, j] = acc
        sq = jnp.sum(acc * acc, axis=-1, keepdims=True)
        ss_ref[slot] = jnp.where(j == 0, sq, ss_ref[slot] + sq)

    @pl.when(i > 0)
    def _():
        prev = 1 - slot
        rstd = lax.rsqrt(ss_ref[prev] * (1.0 / n_total) + EPS)
        o_ref[...] = x_ref[...] + gate_ref[...] * (acc_ref[prev, j] * rstd * gain_ref[...])


def _mm_post(a_list, w, x, gain, gate, cast=()):
    m = x.shape[0]
    splits = tuple(a.shape[1] for a in a_list)
    k, n = w.shape
    bm = min(512, m)
    bn = 1024 if k <= 4096 else 512
    ni = m // bm
    nj = n // bn

    def prev_tile(i, j):
        return (jnp.maximum(i - 1, 0), jnp.where(i == 0, 0, j))

    c_in, c_out, c_shapes = _cast_specs(cast, [(s.shape[2],) for s, _ in cast], nj, (ni + 1) * nj)
    in_specs = [pl.BlockSpec((bm, kk), lambda i, j: (jnp.minimum(i, ni - 1), 0)) for kk in splits]
    in_specs += [pl.BlockSpec((k, bn), lambda i, j: (0, jnp.where(i == ni, nj - 1, j))),
                 pl.BlockSpec((bm, bn), prev_tile),
                 pl.BlockSpec((1, bn), lambda i, j: (0, j)),
                 pl.BlockSpec((1, bn), lambda i, j: (0, j))] + c_in
    res = pl.pallas_call(
        functools.partial(_mm_post_kernel, splits=splits, ni=ni, n_total=n, ncast=len(cast)),
        grid=(ni + 1, nj),
        in_specs=in_specs,
        out_specs=[pl.BlockSpec((bm, bn), prev_tile)] + c_out,
        out_shape=[jax.ShapeDtypeStruct((m, n), F32)] + c_shapes,
        scratch_shapes=[pltpu.VMEM((2, nj, bm, bn), F32), pltpu.VMEM((2, bm, 1), F32)],
        compiler_params=_cparams(("arbitrary", "arbitrary")),
        name="mm_post",
    )(*a_list, w, x, gain, gate, *[s for s, _ in cast])
    return res if cast else res[0]


FFN_HALO = BF16_SUBLANES // 2
FFN_BN = 512


def _ffn_up_kernel(*refs, bm, ni, ncast):
    x_ref, xp_ref, xn_ref, gain_ref, sc_ref, sh_ref, wg_ref, wv_ref, cg_ref, cv_ref = refs[:10]
    cast_src = refs[10:10 + ncast]
    o_ref = refs[10 + ncast]
    cast_dst = refs[11 + ncast:11 + 2 * ncast]
    h_ref = refs[11 + 2 * ncast]
    _cast_plain(cast_src, cast_dst)
    i = pl.program_id(0)
    j = pl.program_id(1)
    hl = FFN_HALO

    @pl.when(j == 0)
    def _():
        gain, sc, sh = gain_ref[...], sc_ref[...], sh_ref[...]
        h_ref[0:bm, :] = _modulate(x_ref[...], gain, sc, sh).astype(BF16)
        hp = _modulate(xp_ref[...], gain, sc, sh)
        hn = _modulate(xn_ref[...], gain, sc, sh)
        halo = jnp.concatenate([jnp.where(i > 0, hp, 0.0), jnp.where(i < ni - 1, hn, 0.0)], axis=0)
        h_ref[bm:bm + 2 * hl, :] = halo.astype(BF16)

    h = h_ref[...]
    row = lax.broadcasted_iota(jnp.int32, (bm, o_ref.shape[1]), 0)

    def conv(w_ref, cw_ref):
        y = _dot(h, w_ref[...])
        ym = y[0:bm, :]
        y_prev = jnp.where(row == 0, y[bm + hl - 1:bm + hl, :], pltpu.roll(ym, 1, 0))
        y_next = jnp.where(row == bm - 1, y[bm + hl:bm + hl + 1, :], pltpu.roll(ym, bm - 1, 0))
        return y_prev * cw_ref[0:1, :] + ym * cw_ref[1:2, :] + y_next * cw_ref[2:3, :]

    gate = conv(wg_ref, cg_ref)
    val = conv(wv_ref, cv_ref)
    o_ref[...] = (gate * jax.nn.sigmoid(gate) * val).astype(BF16)


def _ffn_up(x, gain, scale, shift, w, cw, cast=()):
    m, d = x.shape
    bm = min(512, m)
    ni = m // bm
    bn = FFN_BN
    nt = D_FF // bn
    hl = FFN_HALO
    per = bm // hl
    nhalo = m // hl
    vec = pl.BlockSpec((1, d), lambda i, j: (0, 0))
    c_in, c_out, c_shapes = _cast_specs(cast, [(s.shape[2],) for s, _ in cast], nt, ni * nt)
    res = pl.pallas_call(
        functools.partial(_ffn_up_kernel, bm=bm, ni=ni, ncast=len(cast)),
        grid=(ni, nt),
        in_specs=[pl.BlockSpec((bm, d), lambda i, j: (i, 0)),
                  pl.BlockSpec((hl, d), lambda i, j: (jnp.maximum(i * per - 1, 0), 0)),
                  pl.BlockSpec((hl, d), lambda i, j: (jnp.minimum((i + 1) * per, nhalo - 1), 0)),
                  vec, vec, vec,
                  pl.BlockSpec((d, bn), lambda i, j: (0, j)),
                  pl.BlockSpec((d, bn), lambda i, j: (0, nt + j)),
                  pl.BlockSpec((3, bn), lambda i, j: (0, j)),
                  pl.BlockSpec((3, bn), lambda i, j: (0, nt + j))] + c_in,
        out_specs=[pl.BlockSpec((bm, bn), lambda i, j: (i, j))] + c_out,
        out_shape=[jax.ShapeDtypeStruct((m, D_FF), BF16)] + c_shapes,
        scratch_shapes=[pltpu.VMEM((bm + 2 * hl, d), BF16)],
        compiler_params=_cparams(("arbitrary", "arbitrary")),
        name="ffn_up",
    )(x, x, x, gain, scale, shift, w, w, cw, cw, *[s for s, _ in cast])
    return res if cast else res[0]


def _prep_w_in_rows(wt_l):
    return jnp.concatenate([wt_l[W_IN_OFFS[t]:W_IN_OFFS[t + 1]] for t in W_IN_ORDER], axis=0).astype(BF16)


def _prep_w_decay_rows(wt):
    a = wt[:, W_IN_OFFS[4]:W_IN_OFFS[5]]
    return jnp.pad(a, ((0, 0), (0, LANES - W_IN_SIZES[4]), (0, 0))).astype(BF16)


def _prep_decay(w_dec, b_dec):
    wd = jnp.zeros((2, LANES, D_GLA_K), F32)
    wd = wd.at[0, 0:GLA_RANK].set(w_dec[0]).at[1, GLA_RANK:2 * GLA_RANK].set(w_dec[1])
    return wd, b_dec.reshape(2, 1, D_GLA_K)


def _state_zero():
    return jnp.zeros((N_GLA // 2, 2 * HEAD_DIM, LANES), F32)


def kernel(x, c, ctx, c_ctx, ada_w, ada_b, norm_mix_pre, norm_mix_post, w_in, gla_w_decay, gla_b_decay,
           gla_norm, fourier_w, nat_rpb, w_out, norm_ffn_pre, norm_ffn_post, ffn_w_up, ffn_w_conv, ffn_w_down):
    depth = ada_w.shape[0]
    d = x.shape[-1]
    xs = x[0]
    cs = ctx[0]
    seq = xs.shape[0]
    lc = cs.shape[0]

    mod = _ada_mod(c, c_ctx, ada_w, ada_b)
    cos_x, sin_x = _rope_tables(seq, True)
    cos_c, sin_c = _rope_tables(lc, False)

    w_in_t = jnp.swapaxes(w_in, 1, 2)
    w_at = _prep_w_decay_rows(w_in_t)
    w_cat = _prep_w_in_rows(w_in_t[0])
    side_w_in = (seq // min(512, seq)) * ((N_PB + N_PF) // IN_BN) >= (N_PB + N_PF) // W_IN_CAST_ROWS
    w_out_b = w_out[0].astype(BF16)
    w_up_b = ffn_w_up[0].astype(BF16)
    w_down_b = ffn_w_down[0].astype(BF16)
    tt = _nat_bias_table(nat_rpb)

    for l in range(depth):
        last = l == depth - 1
        mx = [mod[l, 0:1, t * d:(t + 1) * d] for t in range(N_MOD)]
        mc = [mod[l, 1:2, t * d:(t + 1) * d] for t in range(N_MOD)]
        vrow = lambda t: t[l].reshape(1, -1)

        wd2, bd2 = _prep_decay(gla_w_decay[l], gla_b_decay[l])
        wc = _four_weights(fourier_w[l])
        gn = vrow(gla_norm)

        res = _inproj(xs, vrow(norm_mix_pre), mx[1], mx[0], w_cat, w_at, l,
                      cast_src=None if (last or not side_w_in) else (w_in_t, l + 1))
        pbx, pfx, pax = res[:3]
        pbc, pfc, pac = _inproj(cs, vrow(norm_mix_pre), mc[1], mc[0], w_cat, w_at, l)

        gla_c, sc_f, sc_b = _gla_bidir(pfc, pac, pbc, cos_c, sin_c, wd2, bd2, gn, _state_zero(), _state_zero())
        gla_x, _, _ = _gla_bidir(pfx, pax, pbx, cos_x, sin_x, wd2, bd2, gn, sc_f, sc_b)
        four_x = _fourier_mix(pbx, wc)
        nat_x = _nat(pbx, pbc, tt, l)
        xs = _mm_post([four_x, gla_x, nat_x], w_out_b, xs, vrow(norm_mix_post), mx[2])
        if not last:
            four_c = _fourier_mix(pbc, wc)
            nat_c = _ctx_attn(pbc)
            cs = _mm_post([four_c, gla_c, nat_c], w_out_b, cs, vrow(norm_mix_post), mc[2])

        if last:
            act_x = _ffn_up(xs, vrow(norm_ffn_pre), mx[4], mx[3], w_up_b, ffn_w_conv[l])
            xs = _mm_post([act_x], w_down_b, xs, vrow(norm_ffn_post), mx[5])
        else:
            act_x, w_down_n, w_out_n = _ffn_up(xs, vrow(norm_ffn_pre), mx[4], mx[3], w_up_b, ffn_w_conv[l],
                                               cast=((ffn_w_down, l + 1), (w_out, l + 1)))
            xs, w_up_n = _mm_post([act_x], w_down_b, xs, vrow(norm_ffn_post), mx[5], cast=((ffn_w_up, l + 1),))
            act_c = _ffn_up(cs, vrow(norm_ffn_pre), mc[4], mc[3], w_up_b, ffn_w_conv[l])
            cs = _mm_post([act_c], w_down_b, cs, vrow(norm_ffn_post), mc[5])
            w_cat = res[3] if side_w_in else _prep_w_in_rows(w_in_t[l + 1])
            w_out_b, w_up_b, w_down_b = w_out_n, w_up_n, w_down_n
    return xs[None]
```

```python
import functools

import numpy as np
import jax
import jax.numpy as jnp
from jax import lax
from jax.experimental import pallas as pl
from jax.experimental.pallas import tpu as pltpu

F32 = jnp.float32
BF16 = jnp.bfloat16

D_MODEL = 4096
GRID_W = 64
HEAD_DIM = 128
N_FOURIER = 8
N_GLA = 12
N_NAT = 12
D_FOURIER = N_FOURIER * HEAD_DIM
D_GLA = N_GLA * HEAD_DIM
GLA_DK = HEAD_DIM // 2
D_GLA_K = N_GLA * GLA_DK
GLA_RANK = 16
GLA_TAU = 16.0
GLA_CHUNK = 64
D_NAT = N_NAT * HEAD_DIM
NAT_KR = 8
NAT_KC = 16
ROPE_BASE = 10000.0
D_FF = 11 * D_MODEL // 8
N_MOD = 6
EPS = 1e-6
NEG = -1e30

LANES = 128
BF16_SUBLANES = 16
VMEM_LIMIT_MB = 56

PB_V, PB_NQ, PB_NK, PB_NV, PB_F = 0, 1536, 3072, 4608, 6144
N_PB = 7168
PF_Q, PF_K, PF_G = 0, 768, 1536
N_PF = 3072
IN_BN = 1024
GLA_SLABS = 2


def _cparams(sem):
    return pltpu.CompilerParams(dimension_semantics=sem, vmem_limit_bytes=VMEM_LIMIT_MB * 1024 * 1024)


def _dot(a, b):
    return jnp.dot(a, b, preferred_element_type=F32)


def _dot_nt(a, b):
    return lax.dot_general(a, b, (((1,), (1,)), ((), ())), preferred_element_type=F32)


def _dot_tn(a, b):
    return lax.dot_general(a, b, (((0,), (0,)), ((), ())), preferred_element_type=F32)


def _split_bf16(x):
    hi = x.astype(BF16)
    lo = (x - hi.astype(F32)).astype(BF16)
    return hi, lo


def _ada_kernel(cc_ref, w_ref, b_ref, o_ref):
    s = cc_ref[...]
    s = s * jax.nn.sigmoid(s)
    s_hi = s.astype(BF16).astype(F32)
    row = lax.broadcasted_iota(jnp.int32, s.shape, 0)
    lhs = jnp.where(row < 8, s_hi, s - s_hi).astype(BF16)
    w_hi, w_lo = _split_bf16(w_ref[0])
    r = _dot(lhs, w_hi) + _dot(lhs, w_lo)
    o_ref[0] = r[0:8] + r[8:16] + b_ref[0]


def _ada_mod(c, c_ctx, ada_w, ada_b):
    depth, d, n = ada_w.shape
    bn = 512
    cc = jnp.zeros((16, d), F32)
    cc = cc.at[0].set(c[0]).at[1].set(c_ctx).at[8].set(c[0]).at[9].set(c_ctx)
    return pl.pallas_call(
        _ada_kernel,
        grid=(depth, n // bn),
        in_specs=[pl.BlockSpec((16, d), lambda l, j: (0, 0)),
                  pl.BlockSpec((1, d, bn), lambda l, j: (l, 0, j)),
                  pl.BlockSpec((1, 1, bn), lambda l, j: (l, 0, j))],
        out_specs=pl.BlockSpec((1, 8, bn), lambda l, j: (l, 0, j)),
        out_shape=jax.ShapeDtypeStruct((depth, 8, n), F32),
        compiler_params=_cparams(("arbitrary", "arbitrary")),
        name="ada_mod",
    )(cc, ada_w, ada_b.reshape(depth, 1, n))


def _modulate(x, gain, scale, shift):
    ms = jnp.mean(x * x, axis=-1, keepdims=True)
    y = x * lax.rsqrt(ms + EPS) * gain
    return y * (1.0 + scale) + shift


CAST_ROWS = BF16_SUBLANES


def _cast_specs(srcs, out_widths, nj, nsteps):
    in_specs, out_specs, out_shapes = [], [], []
    for (s, l), widths in zip(srcs, out_widths):
        nrow, ncol = s.shape[1], s.shape[2]
        rows = next(r for r in range(CAST_ROWS, nrow + 1, CAST_ROWS) if nrow % r == 0 and nrow // r <= nsteps)
        nchunk = nrow // rows

        def chunk(i, j, nchunk=nchunk):
            return jnp.minimum(i * nj + j, nchunk - 1)

        in_specs.append(pl.BlockSpec((1, rows, ncol), lambda i, j, l=l, chunk=chunk: (l, chunk(i, j), 0)))
        for wd in widths:
            out_specs.append(pl.BlockSpec((rows, wd), lambda i, j, chunk=chunk: (chunk(i, j), 0)))
            out_shapes.append(jax.ShapeDtypeStruct((nrow, wd), BF16))
    return in_specs, out_specs, out_shapes


def _cast_plain(src_refs, dst_refs):
    for s_ref, d_ref in zip(src_refs, dst_refs):
        d_ref[...] = s_ref[0].astype(BF16)


W_IN_SIZES = (D_FOURIER, 2 * D_GLA_K, D_GLA, D_GLA, 2 * GLA_RANK, 3 * D_NAT)
W_IN_OFFS = tuple(int(t) for t in np.concatenate([[0], np.cumsum(W_IN_SIZES)]))
W_IN_ORDER = (2, 5, 0, 1, 3)
W_IN_CAST_ROWS = 32


def _w_in_src_block(c):
    r = W_IN_CAST_ROWS
    src = None
    dst_lo = 0
    for t in W_IN_ORDER:
        n = W_IN_SIZES[t] // r
        here = W_IN_OFFS[t] // r + (c - dst_lo)
        src = here if src is None else jnp.where(c >= dst_lo, here, src)
        dst_lo += n
    return src


def _inproj_kernel(*refs, nb_tiles, cast):
    if cast:
        x_ref, gain_ref, sc_ref, sh_ref, w_ref, wa_ref, src_ref, ob_ref, of_ref, oa_ref, dst_ref, h_ref = refs
        dst_ref[...] = src_ref[0].astype(BF16)
    else:
        x_ref, gain_ref, sc_ref, sh_ref, w_ref, wa_ref, ob_ref, of_ref, oa_ref, h_ref = refs
    j = pl.program_id(1)

    @pl.when(j == 0)
    def _():
        h = _modulate(x_ref[...], gain_ref[...], sc_ref[...], sh_ref[...]).astype(BF16)
        h_ref[...] = h
        wa = wa_ref[0].astype(BF16)
        wa = jnp.concatenate([wa, jnp.zeros((LANES - wa.shape[0], wa.shape[1]), BF16)], axis=0)
        oa_ref[...] = _dot_nt(h, wa)

    r = _dot_nt(h_ref[...], w_ref[...])

    @pl.when(j < nb_tiles)
    def _():
        ob_ref[...] = r.astype(BF16)

    @pl.when(j >= nb_tiles)
    def _():
        of_ref[...] = r


def _inproj(x, gain, scale, shift, wt, wat, l, cast_src=None):
    m, d = x.shape
    bm = min(512, m)
    bn = IN_BN
    nb_tiles = N_PB // bn
    nf_tiles = N_PF // bn
    nj = nb_tiles + nf_tiles
    nrow = N_PB + N_PF
    vec = pl.BlockSpec((1, d), lambda i, j: (0, 0))
    in_specs = [pl.BlockSpec((bm, d), lambda i, j: (i, 0)), vec, vec, vec,
                pl.BlockSpec((bn, d), lambda i, j: (j, 0)),
                pl.BlockSpec((1, W_IN_SIZES[4], d), lambda i, j: (l, W_IN_OFFS[4] // W_IN_SIZES[4], 0))]
    out_specs = [pl.BlockSpec((bm, bn), lambda i, j: (i, jnp.minimum(j, nb_tiles - 1))),
                 pl.BlockSpec((bm, bn), lambda i, j: (i, jnp.maximum(j - nb_tiles, 0))),
                 pl.BlockSpec((bm, LANES), lambda i, j: (i, 0))]
    out_shape = [jax.ShapeDtypeStruct((m, N_PB), BF16), jax.ShapeDtypeStruct((m, N_PF), F32),
                 jax.ShapeDtypeStruct((m, LANES), F32)]
    args = [x, gain, scale, shift, wt, wat]
    if cast_src is not None:
        src, ls = cast_src
        nchunk = nrow // W_IN_CAST_ROWS
        assert nchunk <= (m // bm) * nj

        def chunk(i, j):
            return jnp.minimum(i * nj + j, nchunk - 1)

        in_specs.append(pl.BlockSpec((1, W_IN_CAST_ROWS, d), lambda i, j: (ls, _w_in_src_block(chunk(i, j)), 0)))
        out_specs.append(pl.BlockSpec((W_IN_CAST_ROWS, d), lambda i, j: (chunk(i, j), 0)))
        out_shape.append(jax.ShapeDtypeStruct((nrow, d), BF16))
        args.append(src)
    return pl.pallas_call(
        functools.partial(_inproj_kernel, nb_tiles=nb_tiles, cast=cast_src is not None),
        grid=(m // bm, nj),
        in_specs=in_specs,
        out_specs=out_specs,
        out_shape=out_shape,
        scratch_shapes=[pltpu.VMEM((bm, d), BF16)],
        compiler_params=_cparams(("arbitrary", "arbitrary")),
        name="inproj",
    )(*args)


def _four_w_kernel(cc_ref, sc_ref, fw_ref, o_ref):
    c_hi, c_lo = _split_bf16(cc_ref[...])
    s_hi, s_lo = _split_bf16(sc_ref[...])
    for g in range(N_FOURIER):
        w_hi, w_lo = _split_bf16(fw_ref[g])
        a = _dot(c_hi, w_hi) + _dot(c_hi, w_lo) + _dot(c_lo, w_hi)
        b = _dot(s_hi, w_hi) + _dot(s_hi, w_lo) + _dot(s_lo, w_hi)
        o_ref[g, :, 0:HEAD_DIM] = a
        o_ref[g, :, HEAD_DIM:2 * HEAD_DIM] = -b


def _four_weights(fourier_w_l):
    k = np.arange(HEAD_DIM)
    ang = 2.0 * np.pi * ((k[:, None] * k[None, :]) % HEAD_DIM) / HEAD_DIM
    cc = jnp.asarray(np.cos(ang), F32)
    sc = jnp.asarray(np.sin(ang), F32)
    return pl.pallas_call(
        _four_w_kernel,
        out_shape=jax.ShapeDtypeStruct((N_FOURIER, HEAD_DIM, 2 * HEAD_DIM), F32),
        name="four_w",
    )(cc, sc, fourier_w_l)


def _four_s0_kernel(u_ref, wc_ref, v_ref, *, norm):
    for g in range(N_FOURIER):
        cols = slice(g * HEAD_DIM, (g + 1) * HEAD_DIM)
        y = _dot(u_ref[:, cols], wc_ref[g].astype(BF16)) * norm
        v_ref[0, :, cols] = y[:, 0:HEAD_DIM].astype(BF16)
        v_ref[1, :, cols] = y[:, HEAD_DIM:2 * HEAD_DIM].astype(BF16)


def _four_s1_kernel(m1_ref, v_ref, z_ref):
    z_ref[...] = _dot(m1_ref[...].astype(BF16), v_ref[...]).astype(BF16)


def _four_s2_kernel(t_ref, z_ref, o_ref):
    kb, n2, ch = z_ref.shape[1], z_ref.shape[2], z_ref.shape[3]
    for kk in range(kb):
        z = jnp.concatenate([z_ref[0, kk], z_ref[1, kk]], axis=0)
        o_ref[:, kk * ch:(kk + 1) * ch] = _dot(t_ref[kk].astype(BF16), z).astype(BF16)


@functools.lru_cache(maxsize=None)
def _four_tables(n1, n2):
    length = n1 * n2
    m1 = None
    if n1 > 1:
        j = np.arange(n1)
        ang = 2.0 * np.pi * ((j[:, None] * j[None, :]) % n1) / n1
        cs, ss = np.cos(ang), np.sin(ang)
        m1 = np.block([[cs, ss], [-ss, cs]]).astype(np.float32)
    k1 = np.arange(n1, dtype=np.int64)[:, None, None]
    k2 = np.arange(n2, dtype=np.int64)[None, :, None]
    j2 = np.arange(n2, dtype=np.int64)[None, None, :]
    ang = 2.0 * np.pi * ((j2 * (k1 + n1 * k2)) % length) / length
    t2 = np.concatenate([np.cos(ang), np.sin(ang)], axis=-1).astype(np.float32)
    return m1, t2


def _fourier_mix(pb, wc):
    length = pb.shape[0]
    ch = D_FOURIER
    n2 = length if length <= 512 else 128
    n1 = length // n2
    m1, t2 = _four_tables(n1, n2)
    norm = float(1.0 / np.sqrt(float(length) * HEAD_DIM))
    bm = min(512, length)
    v = pl.pallas_call(
        functools.partial(_four_s0_kernel, norm=norm),
        grid=(length // bm,),
        in_specs=[pl.BlockSpec((bm, ch), lambda i: (i, PB_F // ch)),
                  pl.BlockSpec((N_FOURIER, HEAD_DIM, 2 * HEAD_DIM), lambda i: (0, 0, 0))],
        out_specs=pl.BlockSpec((2, bm, ch), lambda i: (0, i, 0)),
        out_shape=jax.ShapeDtypeStruct((2, length, ch), BF16),
        compiler_params=_cparams(("arbitrary",)),
        name="four_s0",
    )(pb, wc)
    if n1 > 1:
        ncol = n2 * ch
        bc = min(4096, ncol)
        z = pl.pallas_call(
            _four_s1_kernel,
            grid=(ncol // bc,),
            in_specs=[pl.BlockSpec((2 * n1, 2 * n1), lambda i: (0, 0)),
                      pl.BlockSpec((2 * n1, bc), lambda i: (0, i))],
            out_specs=pl.BlockSpec((2 * n1, bc), lambda i: (0, i)),
            out_shape=jax.ShapeDtypeStruct((2 * n1, ncol), BF16),
            compiler_params=_cparams(("arbitrary",)),
            name="four_s1",
        )(jnp.asarray(m1), v.reshape(2 * n1, ncol))
    else:
        z = v
    z4 = z.reshape(2, n1, n2, ch)
    kb = min(4, n1)
    out = pl.pallas_call(
        _four_s2_kernel,
        grid=(n1 // kb,),
        in_specs=[pl.BlockSpec((kb, n2, 2 * n2), lambda i: (i, 0, 0)),
                  pl.BlockSpec((2, kb, n2, ch), lambda i: (0, i, 0, 0))],
        out_specs=pl.BlockSpec((n2, kb * ch), lambda i: (0, i)),
        out_shape=jax.ShapeDtypeStruct((n2, n1 * ch), BF16),
        compiler_params=_cparams(("arbitrary",)),
        name="four_s2",
    )(jnp.asarray(t2), z4)
    return out.reshape(length, ch)


def _rope_tables(length, use_rope):
    if not use_rope:
        return jnp.ones((length, LANES), F32), jnp.zeros((length, LANES), F32)
    pos = jnp.arange(length)
    prow = (pos // GRID_W).astype(F32)
    pcol = (pos % GRID_W).astype(F32)
    half = GLA_DK // 4
    freqs = ROPE_BASE ** (-jnp.arange(half, dtype=F32) / half)
    ar = prow[:, None] * freqs
    ac = pcol[:, None] * freqs
    cos64 = jnp.concatenate([jnp.cos(ar), jnp.cos(ar), jnp.cos(ac), jnp.cos(ac)], axis=-1)
    sin64 = jnp.concatenate([-jnp.sin(ar), jnp.sin(ar), -jnp.sin(ac), jnp.sin(ac)], axis=-1)
    return jnp.concatenate([cos64, cos64], axis=-1), jnp.concatenate([sin64, sin64], axis=-1)


def _rope(x, cos, sin):
    lane = lax.broadcasted_iota(jnp.int32, x.shape, 1)
    first = (lane % 32) < 16
    partner = jnp.where(first, pltpu.roll(x, LANES - 16, 1), pltpu.roll(x, 16, 1))
    return x * cos + partner * sin


def _gla_kernel(*refs, reverse, nchunk, nblk, finalize):
    if finalize:
        (q_ref, k_ref, a_ref, v_ref, cos_ref, sin_ref, wd_ref, bd_ref, tri_ref, s0_ref,
         of_ref, g_ref, gn_ref, o_ref, sfin_ref, st_ref) = refs
    else:
        (q_ref, k_ref, a_ref, v_ref, cos_ref, sin_ref, wd_ref, bd_ref, tri_ref, s0_ref,
         o_ref, sfin_ref, st_ref) = refs
        of_ref = g_ref = gn_ref = None
    jb = pl.program_id(1)

    @pl.when(jb == 0)
    def _():
        st_ref[...] = s0_ref[...]

    c = GLA_CHUNK
    slabs = range(GLA_SLABS)
    prep = []
    for sl in slabs:
        one = slice(sl * LANES, (sl + 1) * LANES)
        prep.append(_gla_prep(q_ref[:, one], k_ref[:, one], a_ref[...], cos_ref[...], sin_ref[...],
                              wd_ref[:, one], bd_ref[:, one], tri_ref[...], reverse=reverse, nchunk=nchunk))
    masks = _gla_masks(reverse)
    states = [st_ref[sl] for sl in slabs]
    order = range(nchunk - 1, -1, -1) if reverse else range(nchunk)
    for ci in order:
        for sl in slabs:
            q_dec, k_inv, dec_cols = prep[sl]
            rows = slice(ci * c, (ci + 1) * c)
            two = slice(2 * sl * LANES, 2 * (sl + 1) * LANES)
            states[sl] = _gla_chunk(q_dec[rows, :], k_inv[rows, :], dec_cols[:, ci:ci + 1], states[sl], masks,
                                    v_ref, of_ref, g_ref, gn_ref, o_ref, rows, two, finalize=finalize)
    for sl in slabs:
        st_ref[sl] = states[sl]

    @pl.when(jb == nblk - 1)
    def _():
        for sl in slabs:
            sfin_ref[sl] = states[sl]


def _gla_prep(q, k, a, cos, sin, wd, bd, tri, *, reverse, nchunk):
    c = GLA_CHUNK
    z = _dot(a.astype(BF16), wd.astype(BF16)) + bd
    g = (jnp.minimum(z, 0.0) - jnp.log(1.0 + jnp.exp(-jnp.abs(z)))) * (1.0 / GLA_TAU)
    g_hi, g_lo = _split_bf16(g)
    bsum = _dot(tri, jnp.concatenate([g_hi, g_lo], axis=1))
    b = bsum[:, 0:LANES] + bsum[:, LANES:2 * LANES]
    q_dec = _rope(q, cos, sin) * (GLA_DK ** -0.5) * jnp.exp(b)
    k_inv = _rope(k, cos, sin) * jnp.exp(-b)
    lasts = [(ci * c if reverse else ci * c + c - 1) for ci in range(nchunk)]
    b_last = jnp.concatenate([b[r:r + 1, :] for r in lasts] +
                             [jnp.zeros((8 - nchunk % 8, LANES), F32)] * (1 if nchunk % 8 else 0), axis=0)
    dec_cols = jnp.exp(b_last.T)
    return q_dec, k_inv, dec_cols


def _gla_masks(reverse):
    c = GLA_CHUNK
    lane = lax.broadcasted_iota(jnp.int32, (c, LANES), 1)
    row = lax.broadcasted_iota(jnp.int32, (c, LANES), 0)
    head0 = lane < GLA_DK
    key_pos = lane % c
    causal = (key_pos >= row) if reverse else (key_pos <= row)
    vhead0 = lax.broadcasted_iota(jnp.int32, (c, 2 * LANES), 1) < LANES
    srow = lax.broadcasted_iota(jnp.int32, (LANES, 2 * HEAD_DIM), 0)
    slane = lax.broadcasted_iota(jnp.int32, (LANES, 2 * HEAD_DIM), 1)
    state_mask = (srow < GLA_DK) == (slane < HEAD_DIM)
    return head0, causal, vhead0, state_mask


def _gla_chunk(q_dec, ki, dec_col, s, masks, v_ref, of_ref, g_ref, gn_ref, o_ref, rows, two, *, finalize):
    head0, causal, vhead0, state_mask = masks
    qd = q_dec.astype(BF16)
    zero = jnp.zeros_like(ki)
    kbd = jnp.concatenate([jnp.where(head0, ki, zero), jnp.where(head0, zero, ki)], axis=0).astype(BF16)
    att = _dot_nt(qd, kbd)
    att = jnp.where(causal, att, 0.0).astype(BF16)
    vc = v_ref[rows, two]
    vzero = jnp.zeros_like(vc)
    vbd = jnp.concatenate([jnp.where(vhead0, vc, vzero), jnp.where(vhead0, vzero, vc)], axis=0)
    o = _dot(att, vbd) + _dot(qd, s.astype(BF16))
    if finalize:
        o = o + of_ref[rows, two]
        gate = g_ref[rows, two]
        gate = gate * jax.nn.sigmoid(gate)
        gn = gn_ref[...]
        outs = []
        for h in range(2):
            oh = o[:, h * HEAD_DIM:(h + 1) * HEAD_DIM]
            ms = jnp.mean(oh * oh, axis=-1, keepdims=True)
            outs.append(oh * lax.rsqrt(ms + EPS) * gn)
        o_ref[rows, two] = (jnp.concatenate(outs, axis=1) * gate).astype(o_ref.dtype)
    else:
        o_ref[rows, two] = o
    u = _dot_tn(ki.astype(BF16), vc)
    return dec_col * (s + jnp.where(state_mask, u, 0.0))


def _gla_tri(tb, reverse):
    t = np.arange(tb)
    same = (t[:, None] // GLA_CHUNK) == (t[None, :] // GLA_CHUNK)
    tri = (t[None, :] >= t[:, None]) if reverse else (t[None, :] <= t[:, None])
    return jnp.asarray((same & tri).astype(np.float32), BF16)


def _gla_pass(pf, pa, pb, cos, sin, wd, bd, s0, reverse, fin=None):
    m = pf.shape[0]
    tb = min(512, m)
    nblk = m // tb
    npair = N_GLA // 2
    finalize = fin is not None

    def rowblk(j):
        return (nblk - 1 - j) if reverse else j

    ns = GLA_SLABS
    w1 = ns * LANES
    w2 = 2 * w1
    in_specs = [
        pl.BlockSpec((tb, w1), lambda h, j: (rowblk(j), PF_Q // w1 + h)),
        pl.BlockSpec((tb, w1), lambda h, j: (rowblk(j), PF_K // w1 + h)),
        pl.BlockSpec((tb, LANES), lambda h, j: (rowblk(j), 0)),
        pl.BlockSpec((tb, w2), lambda h, j: (rowblk(j), PB_V // w2 + h)),
        pl.BlockSpec((tb, LANES), lambda h, j: (rowblk(j), 0)),
        pl.BlockSpec((tb, LANES), lambda h, j: (rowblk(j), 0)),
        pl.BlockSpec((LANES, w1), lambda h, j: (0, h)),
        pl.BlockSpec((1, w1), lambda h, j: (0, h)),
        pl.BlockSpec((tb, tb), lambda h, j: (0, 0)),
        pl.BlockSpec((ns, LANES, 2 * HEAD_DIM), lambda h, j: (h, 0, 0)),
    ]
    args = [pf, pf, pa, pb, cos, sin, wd, bd, _gla_tri(tb, reverse), s0]
    if finalize:
        o_other, gn = fin
        in_specs += [
            pl.BlockSpec((tb, w2), lambda h, j: (rowblk(j), h)),
            pl.BlockSpec((tb, w2), lambda h, j: (rowblk(j), PF_G // w2 + h)),
            pl.BlockSpec((1, HEAD_DIM), lambda h, j: (0, 0)),
        ]
        args += [o_other, pf, gn]
    out_dtype = BF16 if finalize else F32
    return pl.pallas_call(
        functools.partial(_gla_kernel, reverse=reverse, nchunk=tb // GLA_CHUNK, nblk=nblk, finalize=finalize),
        grid=(npair // ns, nblk),
        in_specs=in_specs,
        out_specs=[pl.BlockSpec((tb, w2), lambda h, j: (rowblk(j), h)),
                   pl.BlockSpec((ns, LANES, 2 * HEAD_DIM), lambda h, j: (h, 0, 0))],
        out_shape=[jax.ShapeDtypeStruct((m, D_GLA), out_dtype),
                   jax.ShapeDtypeStruct((npair, LANES, 2 * HEAD_DIM), F32)],
        scratch_shapes=[pltpu.VMEM((ns, LANES, 2 * HEAD_DIM), F32)],
        compiler_params=_cparams(("arbitrary", "arbitrary")),
        name="gla_bwd" if reverse else "gla_fwd",
    )(*args)


def _gla_bidir(pf, pa, pb, cos, sin, wd2, bd2, gn, s0_f, s0_b):
    o_f, s_f = _gla_pass(pf, pa, pb, cos, sin, wd2[0], bd2[0], s0_f, reverse=False)
    out, s_b = _gla_pass(pf, pa, pb, cos, sin, wd2[1], bd2[1], s0_b, reverse=True, fin=(o_f, gn))
    return out, s_f, s_b


@functools.lru_cache(maxsize=None)
def _nat_table_consts():
    qcol = np.arange(GRID_W)
    cstart = np.clip(qcol - NAT_KC // 2, 0, GRID_W - NAT_KC)
    in_win = (qcol[None, :] >= cstart[:, None]) & (qcol[None, :] < cstart[:, None] + NAT_KC)
    col_idx = np.clip(qcol[None, :] - qcol[:, None] + NAT_KC - 1, 0, 2 * NAT_KC - 2)
    width = 2 * NAT_KC
    qc, half, kc = np.meshgrid(qcol, np.arange(2), qcol, indexing="ij")
    pos = (qc * 2 * GRID_W + half * GRID_W + kc).ravel()
    src = (half * width + col_idx[qc, kc]).ravel()
    valid = in_win[qc, kc].ravel()
    expand = np.zeros((2 * width, 2 * GRID_W * GRID_W), np.float32)
    expand[src[valid], pos[valid]] = 1.0
    neg = np.zeros((1, 2 * GRID_W * GRID_W), np.float32)
    neg[0, pos[~valid]] = NEG
    return expand, neg


def _nat_table_kernel(r_ref, e_ref, n_ref, o_ref):
    r = r_ref[...]
    hi = r.astype(BF16)
    rem = r - hi.astype(F32)
    mid = rem.astype(BF16)
    lo = (rem - mid.astype(F32)).astype(BF16)
    e = e_ref[...].astype(BF16)
    o_ref[...] = _dot(hi, e) + _dot(mid, e) + _dot(lo, e) + n_ref[...]


def _nat_bias_table(rpb_all):
    depth, nh = rpb_all.shape[0], rpb_all.shape[1]
    npair = 2 * NAT_KR - 2
    expand, neg = _nat_table_consts()
    rp = jnp.pad(rpb_all.astype(F32), ((0, 0), (0, 0), (0, 0), (0, 1)))
    rows = jnp.concatenate([rp[:, :, 0:npair], rp[:, :, 1:npair + 1]], axis=-1).reshape(depth * nh * npair, -1)
    ncol = expand.shape[1]
    bc = 2048
    out = pl.pallas_call(
        _nat_table_kernel,
        grid=(ncol // bc,),
        in_specs=[pl.BlockSpec(rows.shape, lambda i: (0, 0)),
                  pl.BlockSpec((expand.shape[0], bc), lambda i: (0, i)),
                  pl.BlockSpec((1, bc), lambda i: (0, i))],
        out_specs=pl.BlockSpec((rows.shape[0], bc), lambda i: (0, i)),
        out_shape=jax.ShapeDtypeStruct((rows.shape[0], ncol), F32),
        compiler_params=_cparams(("arbitrary",)),
        name="nat_table",
    )(rows, jnp.asarray(expand), jnp.asarray(neg))
    return out.reshape(depth, nh, npair, GRID_W, 2 * GRID_W)


def _nat_kernel(q_ref, k_ref, v_ref, kc_ref, vc_ref, tt_ref, o_ref, *, rb, rows):
    jb = pl.program_id(1)
    scale = HEAD_DIM ** -0.5
    nloc = NAT_KR * GRID_W
    q_all = q_ref[...]
    s_ctx = _dot_nt(q_all, kc_ref[...]) * scale
    starts = []
    s_rows = []
    for t in range(rb):
        r = jb * rb + t
        rs = jnp.clip(r - NAT_KR // 2, 0, rows - NAT_KR)
        d = r - rs
        start = pl.multiple_of(rs * GRID_W, GRID_W)
        starts.append(start)
        kw = k_ref[pl.ds(start, nloc), :]
        bias = jnp.concatenate([tt_ref[0, 0, NAT_KR - 1 - d + 2 * p] for p in range(NAT_KR // 2)], axis=1)
        s_rows.append(_dot_nt(q_all[t * GRID_W:(t + 1) * GRID_W, :], kw) * scale + bias)
    s_loc = jnp.concatenate(s_rows, axis=0)
    mx = jnp.maximum(jnp.max(s_loc, axis=-1, keepdims=True), jnp.max(s_ctx, axis=-1, keepdims=True))
    e_loc = jnp.exp(s_loc - mx)
    e_ctx = jnp.exp(s_ctx - mx)
    den = jnp.sum(e_loc, axis=-1, keepdims=True) + jnp.sum(e_ctx, axis=-1, keepdims=True)
    e_loc = e_loc.astype(BF16)
    o_ctx = _dot(e_ctx.astype(BF16), vc_ref[...])
    o_rows = [_dot(e_loc[t * GRID_W:(t + 1) * GRID_W, :], v_ref[pl.ds(starts[t], nloc), :]) for t in range(rb)]
    o_ref[...] = ((jnp.concatenate(o_rows, axis=0) + o_ctx) / den).astype(BF16)


def _nat(pb, pbc, tt, l):
    length = pb.shape[0]
    lc = pbc.shape[0]
    rows = length // GRID_W
    rb = min(8, rows)
    npair = 2 * NAT_KR - 2
    return pl.pallas_call(
        functools.partial(_nat_kernel, rb=rb, rows=rows),
        grid=(N_NAT, rows // rb),
        in_specs=[pl.BlockSpec((rb * GRID_W, LANES), lambda h, j: (j, PB_NQ // LANES + h)),
                  pl.BlockSpec((length, LANES), lambda h, j: (0, PB_NK // LANES + h)),
                  pl.BlockSpec((length, LANES), lambda h, j: (0, PB_NV // LANES + h)),
                  pl.BlockSpec((lc, LANES), lambda h, j: (0, PB_NK // LANES + h)),
                  pl.BlockSpec((lc, LANES), lambda h, j: (0, PB_NV // LANES + h)),
                  pl.BlockSpec((1, 1, npair, GRID_W, 2 * GRID_W), lambda h, j: (l, h, 0, 0, 0))],
        out_specs=pl.BlockSpec((rb * GRID_W, LANES), lambda h, j: (j, h)),
        out_shape=jax.ShapeDtypeStruct((length, D_NAT), BF16),
        compiler_params=_cparams(("arbitrary", "arbitrary")),
        name="nat",
    )(pb, pb, pb, pbc, pbc, tt)


def _ctx_attn_kernel(q_ref, k_ref, v_ref, o_ref):
    s = _dot_nt(q_ref[...], k_ref[...]) * (HEAD_DIM ** -0.5)
    e = jnp.exp(s - jnp.max(s, axis=-1, keepdims=True))
    den = jnp.sum(e, axis=-1, keepdims=True)
    o_ref[...] = (_dot(e.astype(BF16), v_ref[...]) / den).astype(BF16)


def _ctx_attn(pbc):
    lc = pbc.shape[0]
    return pl.pallas_call(
        _ctx_attn_kernel,
        grid=(N_NAT,),
        in_specs=[pl.BlockSpec((lc, LANES), lambda h: (0, PB_NQ // LANES + h)),
                  pl.BlockSpec((lc, LANES), lambda h: (0, PB_NK // LANES + h)),
                  pl.BlockSpec((lc, LANES), lambda h: (0, PB_NV // LANES + h))],
        out_specs=pl.BlockSpec((lc, LANES), lambda h: (0, h)),
        out_shape=jax.ShapeDtypeStruct((lc, D_NAT), BF16),
        compiler_params=_cparams(("arbitrary",)),
        name="ctx_attn",
    )(pbc, pbc, pbc)


def _mm_post_kernel(*refs, splits, ni, n_total, ncast):
    na = len(splits)
    a_refs = refs[:na]
    w_ref, x_ref, gain_ref, gate_ref = refs[na:na + 4]
    cast_src = refs[na + 4:na + 4 + ncast]
    o_ref = refs[na + 4 + ncast]
    cast_dst = refs[na + 5 + ncast:na + 5 + 2 * ncast]
    acc_ref, ss_ref = refs[na + 5 + 2 * ncast:]
    _cast_plain(cast_src, cast_dst)
    i = pl.program_id(0)
    j = pl.program_id(1)
    slot = i % 2

    @pl.when(i < ni)
    def _():
        acc = None
        off = 0
        for a_ref, kk in zip(a_refs, splits):
            part = _dot(a_ref[...], w_ref[off:off + kk, :])
            acc = part if acc is None else acc + part
            off += kk
        acc_ref[slot, j] = acc
        sq = jnp.sum(acc * acc, axis=-1, keepdims=True)
        ss_ref[slot] = jnp.where(j == 0, sq, ss_ref[slot] + sq)

    @pl.when(i > 0)
    def _():
        prev = 1 - slot
        rstd = lax.rsqrt(ss_ref[prev] * (1.0 / n_total) + EPS)
        o_ref[...] = x_ref[...] + gate_ref[...] * (acc_ref[prev, j] * rstd * gain_ref[...])


def _mm_post(a_list, w, x, gain, gate, cast=()):
    m = x.shape[0]
    splits = tuple(a.shape[1] for a in a_list)
    k, n = w.shape
    bm = min(512, m)
    bn = 1024 if k <= 4096 else 512
    ni = m // bm
    nj = n // bn

    def prev_tile(i, j):
        return (jnp.maximum(i - 1, 0), jnp.where(i == 0, 0, j))

    c_in, c_out, c_shapes = _cast_specs(cast, [(s.shape[2],) for s, _ in cast], nj, (ni + 1) * nj)
    in_specs = [pl.BlockSpec((bm, kk), lambda i, j: (jnp.minimum(i, ni - 1), 0)) for kk in splits]
    in_specs += [pl.BlockSpec((k, bn), lambda i, j: (0, jnp.where(i == ni, nj - 1, j))),
                 pl.BlockSpec((bm, bn), prev_tile),
                 pl.BlockSpec((1, bn), lambda i, j: (0, j)),
                 pl.BlockSpec((1, bn), lambda i, j: (0, j))] + c_in
    res = pl.pallas_call(
        functools.partial(_mm_post_kernel, splits=splits, ni=ni, n_total=n, ncast=len(cast)),
        grid=(ni + 1, nj),
        in_specs=in_specs,
        out_specs=[pl.BlockSpec((bm, bn), prev_tile)] + c_out,
        out_shape=[jax.ShapeDtypeStruct((m, n), F32)] + c_shapes,
        scratch_shapes=[pltpu.VMEM((2, nj, bm, bn), F32), pltpu.VMEM((2, bm, 1), F32)],
        compiler_params=_cparams(("arbitrary", "arbitrary")),
        name="mm_post",
    )(*a_list, w, x, gain, gate, *[s for s, _ in cast])
    return res if cast else res[0]


FFN_HALO = BF16_SUBLANES // 2
FFN_BN = 512
FFN_PARTS = 1


def _ffn_up_kernel(*refs, bm, ni, ncast):
    x_ref, xp_ref, xn_ref, gain_ref, sc_ref, sh_ref, wg_ref, wv_ref, cg_ref, cv_ref = refs[:10]
    cast_src = refs[10:10 + ncast]
    o_ref = refs[10 + ncast]
    cast_dst = refs[11 + ncast:11 + 2 * ncast]
    h_ref = refs[11 + 2 * ncast]
    _cast_plain(cast_src, cast_dst)
    i = pl.program_id(0)
    j = pl.program_id(1)
    hl = FFN_HALO

    @pl.when(j == 0)
    def _():
        h_ref[0:bm, :] = _modulate(x_ref[...], gain_ref[...], sc_ref[...], sh_ref[...]).astype(BF16)

    @pl.when(j == 0)
    def _():
        gain, sc, sh = gain_ref[...], sc_ref[...], sh_ref[...]
        hp = _modulate(xp_ref[...], gain, sc, sh)
        hn = _modulate(xn_ref[...], gain, sc, sh)
        halo = jnp.concatenate([jnp.where(i > 0, hp, 0.0), jnp.where(i < ni - 1, hn, 0.0)], axis=0)
        h_ref[bm:bm + 2 * hl, :] = halo.astype(BF16)

    h = h_ref[...]
    part = o_ref.shape[1] // FFN_PARTS
    row = lax.broadcasted_iota(jnp.int32, (bm, part), 0)

    def conv(w_ref, cw_ref, cols):
        y = _dot(h, w_ref[:, cols])
        ym = y[0:bm, :]
        y_prev = jnp.where(row == 0, y[bm + hl - 1:bm + hl, :], pltpu.roll(ym, 1, 0))
        y_next = jnp.where(row == bm - 1, y[bm + hl:bm + hl + 1, :], pltpu.roll(ym, bm - 1, 0))
        return y_prev * cw_ref[0:1, cols] + ym * cw_ref[1:2, cols] + y_next * cw_ref[2:3, cols]

    for p in range(FFN_PARTS):
        cols = slice(p * part, (p + 1) * part)
        gate = conv(wg_ref, cg_ref, cols)
        val = conv(wv_ref, cv_ref, cols)
        o_ref[:, cols] = (gate * jax.nn.sigmoid(gate) * val).astype(BF16)


def _ffn_up(x, gain, scale, shift, w, cw, cast=()):
    m, d = x.shape
    bm = min(512, m)
    ni = m // bm
    bn = FFN_BN
    nt = D_FF // bn
    hl = FFN_HALO
    per = bm // hl
    nhalo = m // hl
    vec = pl.BlockSpec((1, d), lambda i, j: (0, 0))
    c_in, c_out, c_shapes = _cast_specs(cast, [(s.shape[2],) for s, _ in cast], nt, ni * nt)
    res = pl.pallas_call(
        functools.partial(_ffn_up_kernel, bm=bm, ni=ni, ncast=len(cast)),
        grid=(ni, nt),
        in_specs=[pl.BlockSpec((bm, d), lambda i, j: (i, 0)),
                  pl.BlockSpec((hl, d), lambda i, j: (jnp.maximum(i * per - 1, 0), 0)),
                  pl.BlockSpec((hl, d), lambda i, j: (jnp.minimum((i + 1) * per, nhalo - 1), 0)),
                  vec, vec, vec,
                  pl.BlockSpec((d, bn), lambda i, j: (0, j)),
                  pl.BlockSpec((d, bn), lambda i, j: (0, nt + j)),
                  pl.BlockSpec((3, bn), lambda i, j: (0, j)),
                  pl.BlockSpec((3, bn), lambda i, j: (0, nt + j))] + c_in,
        out_specs=[pl.BlockSpec((bm, bn), lambda i, j: (i, j))] + c_out,
        out_shape=[jax.ShapeDtypeStruct((m, D_FF), BF16)] + c_shapes,
        scratch_shapes=[pltpu.VMEM((bm + 2 * hl, d), BF16)],
        compiler_params=_cparams(("arbitrary", "arbitrary")),
        name="ffn_up",
    )(x, x, x, gain, scale, shift, w, w, cw, cw, *[s for s, _ in cast])
    return res if cast else res[0]


def _prep_w_in_rows(wt_l):
    return jnp.concatenate([wt_l[W_IN_OFFS[t]:W_IN_OFFS[t + 1]] for t in W_IN_ORDER], axis=0).astype(BF16)


def _prep_decay(w_dec, b_dec):
    wd = jnp.zeros((2, LANES, D_GLA_K), F32)
    wd = wd.at[0, 0:GLA_RANK].set(w_dec[0]).at[1, GLA_RANK:2 * GLA_RANK].set(w_dec[1])
    return wd, b_dec.reshape(2, 1, D_GLA_K)


def _state_zero():
    return jnp.zeros((N_GLA // 2, LANES, 2 * HEAD_DIM), F32)


def kernel(x, c, ctx, c_ctx, ada_w, ada_b, norm_mix_pre, norm_mix_post, w_in, gla_w_decay, gla_b_decay,
           gla_norm, fourier_w, nat_rpb, w_out, norm_ffn_pre, norm_ffn_post, ffn_w_up, ffn_w_conv, ffn_w_down):
    depth = ada_w.shape[0]
    d = x.shape[-1]
    xs = x[0]
    cs = ctx[0]
    seq = xs.shape[0]
    lc = cs.shape[0]

    mod = _ada_mod(c, c_ctx, ada_w, ada_b)
    cos_x, sin_x = _rope_tables(seq, True)
    cos_c, sin_c = _rope_tables(lc, False)

    w_in_t = jnp.swapaxes(w_in, 1, 2)
    w_at = w_in_t
    w_cat = _prep_w_in_rows(w_in_t[0])
    side_w_in = (seq // min(512, seq)) * ((N_PB + N_PF) // IN_BN) >= (N_PB + N_PF) // W_IN_CAST_ROWS
    w_out_b = w_out[0].astype(BF16)
    w_up_b = ffn_w_up[0].astype(BF16)
    w_down_b = ffn_w_down[0].astype(BF16)
    tt = _nat_bias_table(nat_rpb)

    for l in range(depth):
        last = l == depth - 1
        mx = [mod[l, 0:1, t * d:(t + 1) * d] for t in range(N_MOD)]
        mc = [mod[l, 1:2, t * d:(t + 1) * d] for t in range(N_MOD)]
        vrow = lambda t: t[l].reshape(1, -1)

        wd2, bd2 = _prep_decay(gla_w_decay[l], gla_b_decay[l])
        wc = _four_weights(fourier_w[l])
        gn = vrow(gla_norm)

        res = _inproj(xs, vrow(norm_mix_pre), mx[1], mx[0], w_cat, w_at, l,
                      cast_src=None if (last or not side_w_in) else (w_in_t, l + 1))
        pbx, pfx, pax = res[:3]
        pbc, pfc, pac = _inproj(cs, vrow(norm_mix_pre), mc[1], mc[0], w_cat, w_at, l)

        gla_c, sc_f, sc_b = _gla_bidir(pfc, pac, pbc, cos_c, sin_c, wd2, bd2, gn, _state_zero(), _state_zero())
        gla_x, _, _ = _gla_bidir(pfx, pax, pbx, cos_x, sin_x, wd2, bd2, gn, sc_f, sc_b)
        four_x = _fourier_mix(pbx, wc)
        nat_x = _nat(pbx, pbc, tt, l)
        xs = _mm_post([four_x, gla_x, nat_x], w_out_b, xs, vrow(norm_mix_post), mx[2])
        if not last:
            four_c = _fourier_mix(pbc, wc)
            nat_c = _ctx_attn(pbc)
            cs = _mm_post([four_c, gla_c, nat_c], w_out_b, cs, vrow(norm_mix_post), mc[2])

        if last:
            act_x = _ffn_up(xs, vrow(norm_ffn_pre), mx[4], mx[3], w_up_b, ffn_w_conv[l])
            xs = _mm_post([act_x], w_down_b, xs, vrow(norm_ffn_post), mx[5])
        else:
            act_x, w_down_n, w_out_n = _ffn_up(xs, vrow(norm_ffn_pre), mx[4], mx[3], w_up_b, ffn_w_conv[l],
                                               cast=((ffn_w_down, l + 1), (w_out, l + 1)))
            xs, w_up_n = _mm_post([act_x], w_down_b, xs, vrow(norm_ffn_post), mx[5], cast=((ffn_w_up, l + 1),))
            act_c = _ffn_up(cs, vrow(norm_ffn_pre), mc[4], mc[3], w_up_b, ffn_w_conv[l])
            cs = _mm_post([act_c], w_down_b, cs, vrow(norm_ffn_post), mc[5])
            w_cat = res[3] if side_w_in else _prep_w_in_rows(w_in_t[l + 1])
            w_out_b, w_up_b, w_down_b = w_out_n, w_up_n, w_down_n
    return xs[None]
```

```python
import functools

import numpy as np
import jax
import jax.numpy as jnp
from jax import lax
from jax.experimental import pallas as pl
from jax.experimental.pallas import tpu as pltpu

F32 = jnp.float32
BF16 = jnp.bfloat16

D_MODEL = 4096
GRID_W = 64
HEAD_DIM = 128
N_FOURIER = 8
N_GLA = 12
N_NAT = 12
D_FOURIER = N_FOURIER * HEAD_DIM
D_GLA = N_GLA * HEAD_DIM
GLA_DK = HEAD_DIM // 2
D_GLA_K = N_GLA * GLA_DK
GLA_RANK = 16
GLA_TAU = 16.0
GLA_CHUNK = 64
D_NAT = N_NAT * HEAD_DIM
NAT_KR = 8
NAT_KC = 16
ROPE_BASE = 10000.0
D_FF = 11 * D_MODEL // 8
N_MOD = 6
EPS = 1e-6
NEG = -1e30

LANES = 128
BF16_SUBLANES = 16
VMEM_LIMIT_MB = 56

PB_V, PB_NQ, PB_NK, PB_NV, PB_F = 0, 1536, 3072, 4608, 6144
N_PB = 7168
PF_Q, PF_K, PF_G = 0, 768, 1536
N_PF = 3072
IN_BN = 1024
GLA_SLABS = 2


def _cparams(sem):
    return pltpu.CompilerParams(dimension_semantics=sem, vmem_limit_bytes=VMEM_LIMIT_MB * 1024 * 1024)


def _dot(a, b):
    return jnp.dot(a, b, preferred_element_type=F32)


def _dot_nt(a, b):
    return lax.dot_general(a, b, (((1,), (1,)), ((), ())), preferred_element_type=F32)


def _dot_tn(a, b):
    return lax.dot_general(a, b, (((0,), (0,)), ((), ())), preferred_element_type=F32)


def _split_bf16(x):
    hi = x.astype(BF16)
    lo = (x - hi.astype(F32)).astype(BF16)
    return hi, lo


def _ada_kernel(cc_ref, w_ref, b_ref, o_ref):
    s = cc_ref[...]
    s = s * jax.nn.sigmoid(s)
    s_hi = s.astype(BF16).astype(F32)
    row = lax.broadcasted_iota(jnp.int32, s.shape, 0)
    lhs = jnp.where(row < 8, s_hi, s - s_hi).astype(BF16)
    w_hi, w_lo = _split_bf16(w_ref[0])
    r = _dot(lhs, w_hi) + _dot(lhs, w_lo)
    o_ref[0] = r[0:8] + r[8:16] + b_ref[0]


def _ada_mod(c, c_ctx, ada_w, ada_b):
    depth, d, n = ada_w.shape
    bn = 512
    cc = jnp.zeros((16, d), F32)
    cc = cc.at[0].set(c[0]).at[1].set(c_ctx).at[8].set(c[0]).at[9].set(c_ctx)
    return pl.pallas_call(
        _ada_kernel,
        grid=(depth, n // bn),
        in_specs=[pl.BlockSpec((16, d), lambda l, j: (0, 0)),
                  pl.BlockSpec((1, d, bn), lambda l, j: (l, 0, j)),
                  pl.BlockSpec((1, 1, bn), lambda l, j: (l, 0, j))],
        out_specs=pl.BlockSpec((1, 8, bn), lambda l, j: (l, 0, j)),
        out_shape=jax.ShapeDtypeStruct((depth, 8, n), F32),
        compiler_params=_cparams(("arbitrary", "arbitrary")),
        name="ada_mod",
    )(cc, ada_w, ada_b.reshape(depth, 1, n))


def _modulate(x, gain, scale, shift):
    ms = jnp.mean(x * x, axis=-1, keepdims=True)
    y = x * lax.rsqrt(ms + EPS) * gain
    return y * (1.0 + scale) + shift


CAST_ROWS = BF16_SUBLANES


def _cast_specs(srcs, out_widths, nj, nsteps):
    in_specs, out_specs, out_shapes = [], [], []
    for (s, l), widths in zip(srcs, out_widths):
        nrow, ncol = s.shape[1], s.shape[2]
        rows = next(r for r in range(CAST_ROWS, nrow + 1, CAST_ROWS) if nrow % r == 0 and nrow // r <= nsteps)
        nchunk = nrow // rows

        def chunk(i, j, nchunk=nchunk):
            return jnp.minimum(i * nj + j, nchunk - 1)

        in_specs.append(pl.BlockSpec((1, rows, ncol), lambda i, j, l=l, chunk=chunk: (l, chunk(i, j), 0)))
        for wd in widths:
            out_specs.append(pl.BlockSpec((rows, wd), lambda i, j, chunk=chunk: (chunk(i, j), 0)))
            out_shapes.append(jax.ShapeDtypeStruct((nrow, wd), BF16))
    return in_specs, out_specs, out_shapes


def _cast_plain(src_refs, dst_refs):
    for s_ref, d_ref in zip(src_refs, dst_refs):
        d_ref[...] = s_ref[0].astype(BF16)


W_IN_SIZES = (D_FOURIER, 2 * D_GLA_K, D_GLA, D_GLA, 2 * GLA_RANK, 3 * D_NAT)
W_IN_OFFS = tuple(int(t) for t in np.concatenate([[0], np.cumsum(W_IN_SIZES)]))
W_IN_ORDER = (2, 5, 0, 1, 3)
W_IN_CAST_ROWS = 32


def _w_in_src_block(c):
    r = W_IN_CAST_ROWS
    src = None
    dst_lo = 0
    for t in W_IN_ORDER:
        n = W_IN_SIZES[t] // r
        here = W_IN_OFFS[t] // r + (c - dst_lo)
        src = here if src is None else jnp.where(c >= dst_lo, here, src)
        dst_lo += n
    return src


def _inproj_kernel(*refs, nb_tiles, cast):
    if cast:
        x_ref, gain_ref, sc_ref, sh_ref, w_ref, wa_ref, src_ref, ob_ref, of_ref, oa_ref, dst_ref, h_ref = refs
        dst_ref[...] = src_ref[0].astype(BF16)
    else:
        x_ref, gain_ref, sc_ref, sh_ref, w_ref, wa_ref, ob_ref, of_ref, oa_ref, h_ref = refs
    j = pl.program_id(1)

    @pl.when(j == 0)
    def _():
        h = _modulate(x_ref[...], gain_ref[...], sc_ref[...], sh_ref[...]).astype(BF16)
        h_ref[...] = h
        wa = wa_ref[0].astype(BF16)
        wa = jnp.concatenate([wa, jnp.zeros((LANES - wa.shape[0], wa.shape[1]), BF16)], axis=0)
        oa_ref[...] = _dot_nt(h, wa)

    r = _dot_nt(h_ref[...], w_ref[...])

    @pl.when(j < nb_tiles)
    def _():
        ob_ref[...] = r.astype(BF16)

    @pl.when(j >= nb_tiles)
    def _():
        of_ref[...] = r


def _inproj(x, gain, scale, shift, wt, wat, l, cast_src=None):
    m, d = x.shape
    bm = min(512, m)
    bn = IN_BN
    nb_tiles = N_PB // bn
    nf_tiles = N_PF // bn
    nj = nb_tiles + nf_tiles
    nrow = N_PB + N_PF
    vec = pl.BlockSpec((1, d), lambda i, j: (0, 0))
    in_specs = [pl.BlockSpec((bm, d), lambda i, j: (i, 0)), vec, vec, vec,
                pl.BlockSpec((bn, d), lambda i, j: (j, 0)),
                pl.BlockSpec((1, W_IN_SIZES[4], d), lambda i, j: (l, W_IN_OFFS[4] // W_IN_SIZES[4], 0))]
    out_specs = [pl.BlockSpec((bm, bn), lambda i, j: (i, jnp.minimum(j, nb_tiles - 1))),
                 pl.BlockSpec((bm, bn), lambda i, j: (i, jnp.maximum(j - nb_tiles, 0))),
                 pl.BlockSpec((bm, LANES), lambda i, j: (i, 0))]
    out_shape = [jax.ShapeDtypeStruct((m, N_PB), BF16), jax.ShapeDtypeStruct((m, N_PF), F32),
                 jax.ShapeDtypeStruct((m, LANES), F32)]
    args = [x, gain, scale, shift, wt, wat]
    if cast_src is not None:
        src, ls = cast_src
        nchunk = nrow // W_IN_CAST_ROWS
        assert nchunk <= (m // bm) * nj

        def chunk(i, j):
            return jnp.minimum(i * nj + j, nchunk - 1)

        in_specs.append(pl.BlockSpec((1, W_IN_CAST_ROWS, d), lambda i, j: (ls, _w_in_src_block(chunk(i, j)), 0)))
        out_specs.append(pl.BlockSpec((W_IN_CAST_ROWS, d), lambda i, j: (chunk(i, j), 0)))
        out_shape.append(jax.ShapeDtypeStruct((nrow, d), BF16))
        args.append(src)
    return pl.pallas_call(
        functools.partial(_inproj_kernel, nb_tiles=nb_tiles, cast=cast_src is not None),
        grid=(m // bm, nj),
        in_specs=in_specs,
        out_specs=out_specs,
        out_shape=out_shape,
        scratch_shapes=[pltpu.VMEM((bm, d), BF16)],
        compiler_params=_cparams(("arbitrary", "arbitrary")),
        name="inproj",
    )(*args)


def _four_w_kernel(cc_ref, sc_ref, fw_ref, o_ref):
    c_hi, c_lo = _split_bf16(cc_ref[...])
    s_hi, s_lo = _split_bf16(sc_ref[...])
    for g in range(N_FOURIER):
        w_hi, w_lo = _split_bf16(fw_ref[g])
        a = _dot(c_hi, w_hi) + _dot(c_hi, w_lo) + _dot(c_lo, w_hi)
        b = _dot(s_hi, w_hi) + _dot(s_hi, w_lo) + _dot(s_lo, w_hi)
        o_ref[g, :, 0:HEAD_DIM] = a
        o_ref[g, :, HEAD_DIM:2 * HEAD_DIM] = -b


def _four_weights(fourier_w_l):
    k = np.arange(HEAD_DIM)
    ang = 2.0 * np.pi * ((k[:, None] * k[None, :]) % HEAD_DIM) / HEAD_DIM
    cc = jnp.asarray(np.cos(ang), F32)
    sc = jnp.asarray(np.sin(ang), F32)
    return pl.pallas_call(
        _four_w_kernel,
        out_shape=jax.ShapeDtypeStruct((N_FOURIER, HEAD_DIM, 2 * HEAD_DIM), F32),
        name="four_w",
    )(cc, sc, fourier_w_l)


def _four_s0_kernel(u_ref, wc_ref, v_ref, *, norm):
    for g in range(N_FOURIER):
        cols = slice(g * HEAD_DIM, (g + 1) * HEAD_DIM)
        y = _dot(u_ref[:, cols], wc_ref[g].astype(BF16)) * norm
        v_ref[0, :, cols] = y[:, 0:HEAD_DIM].astype(BF16)
        v_ref[1, :, cols] = y[:, HEAD_DIM:2 * HEAD_DIM].astype(BF16)


def _four_s1_kernel(m1_ref, v_ref, z_ref):
    z_ref[...] = _dot(m1_ref[...].astype(BF16), v_ref[...]).astype(BF16)


def _four_s2_kernel(t_ref, z_ref, o_ref):
    kb, n2, ch = z_ref.shape[1], z_ref.shape[2], z_ref.shape[3]
    for kk in range(kb):
        z = jnp.concatenate([z_ref[0, kk], z_ref[1, kk]], axis=0)
        o_ref[:, kk * ch:(kk + 1) * ch] = _dot(t_ref[kk].astype(BF16), z).astype(BF16)


@functools.lru_cache(maxsize=None)
def _four_tables(n1, n2):
    length = n1 * n2
    m1 = None
    if n1 > 1:
        j = np.arange(n1)
        ang = 2.0 * np.pi * ((j[:, None] * j[None, :]) % n1) / n1
        cs, ss = np.cos(ang), np.sin(ang)
        m1 = np.block([[cs, ss], [-ss, cs]]).astype(np.float32)
    k1 = np.arange(n1, dtype=np.int64)[:, None, None]
    k2 = np.arange(n2, dtype=np.int64)[None, :, None]
    j2 = np.arange(n2, dtype=np.int64)[None, None, :]
    ang = 2.0 * np.pi * ((j2 * (k1 + n1 * k2)) % length) / length
    t2 = np.concatenate([np.cos(ang), np.sin(ang)], axis=-1).astype(np.float32)
    return m1, t2


def _fourier_mix(pb, wc):
    length = pb.shape[0]
    ch = D_FOURIER
    n2 = length if length <= 512 else 128
    n1 = length // n2
    m1, t2 = _four_tables(n1, n2)
    norm = float(1.0 / np.sqrt(float(length) * HEAD_DIM))
    bm = min(512, length)
    v = pl.pallas_call(
        functools.partial(_four_s0_kernel, norm=norm),
        grid=(length // bm,),
        in_specs=[pl.BlockSpec((bm, ch), lambda i: (i, PB_F // ch)),
                  pl.BlockSpec((N_FOURIER, HEAD_DIM, 2 * HEAD_DIM), lambda i: (0, 0, 0))],
        out_specs=pl.BlockSpec((2, bm, ch), lambda i: (0, i, 0)),
        out_shape=jax.ShapeDtypeStruct((2, length, ch), BF16),
        compiler_params=_cparams(("arbitrary",)),
        name="four_s0",
    )(pb, wc)
    if n1 > 1:
        ncol = n2 * ch
        bc = min(4096, ncol)
        z = pl.pallas_call(
            _four_s1_kernel,
            grid=(ncol // bc,),
            in_specs=[pl.BlockSpec((2 * n1, 2 * n1), lambda i: (0, 0)),
                      pl.BlockSpec((2 * n1, bc), lambda i: (0, i))],
            out_specs=pl.BlockSpec((2 * n1, bc), lambda i: (0, i)),
            out_shape=jax.ShapeDtypeStruct((2 * n1, ncol), BF16),
            compiler_params=_cparams(("arbitrary",)),
            name="four_s1",
        )(jnp.asarray(m1), v.reshape(2 * n1, ncol))
    else:
        z = v
    z4 = z.reshape(2, n1, n2, ch)
    kb = min(4, n1)
    out = pl.pallas_call(
        _four_s2_kernel,
        grid=(n1 // kb,),
        in_specs=[pl.BlockSpec((kb, n2, 2 * n2), lambda i: (i, 0, 0)),
                  pl.BlockSpec((2, kb, n2, ch), lambda i: (0, i, 0, 0))],
        out_specs=pl.BlockSpec((n2, kb * ch), lambda i: (0, i)),
        out_shape=jax.ShapeDtypeStruct((n2, n1 * ch), BF16),
        compiler_params=_cparams(("arbitrary",)),
        name="four_s2",
    )(jnp.asarray(t2), z4)
    return out.reshape(length, ch)


def _rope_tables(length, use_rope):
    if not use_rope:
        return jnp.ones((length, LANES), F32), jnp.zeros((length, LANES), F32)
    pos = jnp.arange(length)
    prow = (pos // GRID_W).astype(F32)
    pcol = (pos % GRID_W).astype(F32)
    half = GLA_DK // 4
    freqs = ROPE_BASE ** (-jnp.arange(half, dtype=F32) / half)
    ar = prow[:, None] * freqs
    ac = pcol[:, None] * freqs
    cos64 = jnp.concatenate([jnp.cos(ar), jnp.cos(ar), jnp.cos(ac), jnp.cos(ac)], axis=-1)
    sin64 = jnp.concatenate([-jnp.sin(ar), jnp.sin(ar), -jnp.sin(ac), jnp.sin(ac)], axis=-1)
    return jnp.concatenate([cos64, cos64], axis=-1), jnp.concatenate([sin64, sin64], axis=-1)


def _rope(x, cos, sin):
    lane = lax.broadcasted_iota(jnp.int32, x.shape, 1)
    first = (lane % 32) < 16
    partner = jnp.where(first, pltpu.roll(x, LANES - 16, 1), pltpu.roll(x, 16, 1))
    return x * cos + partner * sin


def _gla_kernel(*refs, reverse, nchunk, nblk, finalize):
    if finalize:
        (q_ref, k_ref, a_ref, v_ref, cos_ref, sin_ref, wd_ref, bd_ref, tri_ref, s0_ref,
         of_ref, g_ref, gn_ref, o_ref, sfin_ref, st_ref) = refs
    else:
        (q_ref, k_ref, a_ref, v_ref, cos_ref, sin_ref, wd_ref, bd_ref, tri_ref, s0_ref,
         o_ref, sfin_ref, st_ref) = refs
        of_ref = g_ref = gn_ref = None
    jb = pl.program_id(1)

    @pl.when(jb == 0)
    def _():
        st_ref[...] = s0_ref[...]

    c = GLA_CHUNK
    slabs = range(GLA_SLABS)
    prep = []
    for sl in slabs:
        one = slice(sl * LANES, (sl + 1) * LANES)
        prep.append(_gla_prep(q_ref[:, one], k_ref[:, one], a_ref[...], cos_ref[...], sin_ref[...],
                              wd_ref[:, one], bd_ref[:, one], tri_ref[...], reverse=reverse, nchunk=nchunk))
    masks = _gla_masks(reverse)
    states = [st_ref[sl] for sl in slabs]
    order = range(nchunk - 1, -1, -1) if reverse else range(nchunk)
    for ci in order:
        for sl in slabs:
            q_dec, k_inv, dec_cols = prep[sl]
            rows = slice(ci * c, (ci + 1) * c)
            two = slice(2 * sl * LANES, 2 * (sl + 1) * LANES)
            states[sl] = _gla_chunk(q_dec[rows, :], k_inv[rows, :], dec_cols[:, ci:ci + 1], states[sl], masks,
                                    v_ref, of_ref, g_ref, gn_ref, o_ref, rows, two, finalize=finalize)
    for sl in slabs:
        st_ref[sl] = states[sl]

    @pl.when(jb == nblk - 1)
    def _():
        for sl in slabs:
            sfin_ref[sl] = states[sl]


def _gla_prep(q, k, a, cos, sin, wd, bd, tri, *, reverse, nchunk):
    c = GLA_CHUNK
    z = _dot(a.astype(BF16), wd.astype(BF16)) + bd
    g = (jnp.minimum(z, 0.0) - jnp.log(1.0 + jnp.exp(-jnp.abs(z)))) * (1.0 / GLA_TAU)
    g_hi, g_lo = _split_bf16(g)
    bsum = _dot(tri, jnp.concatenate([g_hi, g_lo], axis=1))
    b = bsum[:, 0:LANES] + bsum[:, LANES:2 * LANES]
    q_dec = _rope(q, cos, sin) * (GLA_DK ** -0.5) * jnp.exp(b)
    k_inv = _rope(k, cos, sin) * jnp.exp(-b)
    lasts = [(ci * c if reverse else ci * c + c - 1) for ci in range(nchunk)]
    b_last = jnp.concatenate([b[r:r + 1, :] for r in lasts] +
                             [jnp.zeros((8 - nchunk % 8, LANES), F32)] * (1 if nchunk % 8 else 0), axis=0)
    dec_cols = jnp.exp(b_last.T)
    return q_dec, k_inv, dec_cols


def _gla_masks(reverse):
    c = GLA_CHUNK
    lane = lax.broadcasted_iota(jnp.int32, (c, LANES), 1)
    row = lax.broadcasted_iota(jnp.int32, (c, LANES), 0)
    head0 = lane < GLA_DK
    key_pos = lane % c
    causal = (key_pos >= row) if reverse else (key_pos <= row)
    vhead0 = lax.broadcasted_iota(jnp.int32, (c, 2 * LANES), 1) < LANES
    srow = lax.broadcasted_iota(jnp.int32, (LANES, 2 * HEAD_DIM), 0)
    slane = lax.broadcasted_iota(jnp.int32, (LANES, 2 * HEAD_DIM), 1)
    state_mask = (srow < GLA_DK) == (slane < HEAD_DIM)
    return head0, causal, vhead0, state_mask


def _gla_chunk(q_dec, ki, dec_col, s, masks, v_ref, of_ref, g_ref, gn_ref, o_ref, rows, two, *, finalize):
    head0, causal, vhead0, state_mask = masks
    qd = q_dec.astype(BF16)
    zero = jnp.zeros_like(ki)
    kbd = jnp.concatenate([jnp.where(head0, ki, zero), jnp.where(head0, zero, ki)], axis=0).astype(BF16)
    att = _dot_nt(qd, kbd)
    att = jnp.where(causal, att, 0.0).astype(BF16)
    vc = v_ref[rows, two]
    vzero = jnp.zeros_like(vc)
    vbd = jnp.concatenate([jnp.where(vhead0, vc, vzero), jnp.where(vhead0, vzero, vc)], axis=0)
    o = _dot(att, vbd) + _dot(qd, s.astype(BF16))
    if finalize:
        o = o + of_ref[rows, two]
        gate = g_ref[rows, two]
        gate = gate * jax.nn.sigmoid(gate)
        gn = gn_ref[...]
        outs = []
        for h in range(2):
            oh = o[:, h * HEAD_DIM:(h + 1) * HEAD_DIM]
            ms = jnp.mean(oh * oh, axis=-1, keepdims=True)
            outs.append(oh * lax.rsqrt(ms + EPS) * gn)
        o_ref[rows, two] = (jnp.concatenate(outs, axis=1) * gate).astype(o_ref.dtype)
    else:
        o_ref[rows, two] = o
    u = _dot_tn(ki.astype(BF16), vc)
    return dec_col * (s + jnp.where(state_mask, u, 0.0))


def _gla_tri(tb, reverse):
    t = np.arange(tb)
    same = (t[:, None] // GLA_CHUNK) == (t[None, :] // GLA_CHUNK)
    tri = (t[None, :] >= t[:, None]) if reverse else (t[None, :] <= t[:, None])
    return jnp.asarray((same & tri).astype(np.float32), BF16)


def _gla_pass(pf, pa, pb, cos, sin, wd, bd, s0, reverse, fin=None):
    m = pf.shape[0]
    tb = min(512, m)
    nblk = m // tb
    npair = N_GLA // 2
    finalize = fin is not None

    def rowblk(j):
        return (nblk - 1 - j) if reverse else j

    ns = GLA_SLABS
    w1 = ns * LANES
    w2 = 2 * w1
    in_specs = [
        pl.BlockSpec((tb, w1), lambda h, j: (rowblk(j), PF_Q // w1 + h)),
        pl.BlockSpec((tb, w1), lambda h, j: (rowblk(j), PF_K // w1 + h)),
        pl.BlockSpec((tb, LANES), lambda h, j: (rowblk(j), 0)),
        pl.BlockSpec((tb, w2), lambda h, j: (rowblk(j), PB_V // w2 + h)),
        pl.BlockSpec((tb, LANES), lambda h, j: (rowblk(j), 0)),
        pl.BlockSpec((tb, LANES), lambda h, j: (rowblk(j), 0)),
        pl.BlockSpec((LANES, w1), lambda h, j: (0, h)),
        pl.BlockSpec((1, w1), lambda h, j: (0, h)),
        pl.BlockSpec((tb, tb), lambda h, j: (0, 0)),
        pl.BlockSpec((ns, LANES, 2 * HEAD_DIM), lambda h, j: (h, 0, 0)),
    ]
    args = [pf, pf, pa, pb, cos, sin, wd, bd, _gla_tri(tb, reverse), s0]
    if finalize:
        o_other, gn = fin
        in_specs += [
            pl.BlockSpec((tb, w2), lambda h, j: (rowblk(j), h)),
            pl.BlockSpec((tb, w2), lambda h, j: (rowblk(j), PF_G // w2 + h)),
            pl.BlockSpec((1, HEAD_DIM), lambda h, j: (0, 0)),
        ]
        args += [o_other, pf, gn]
    out_dtype = BF16 if finalize else F32
    return pl.pallas_call(
        functools.partial(_gla_kernel, reverse=reverse, nchunk=tb // GLA_CHUNK, nblk=nblk, finalize=finalize),
        grid=(npair // ns, nblk),
        in_specs=in_specs,
        out_specs=[pl.BlockSpec((tb, w2), lambda h, j: (rowblk(j), h)),
                   pl.BlockSpec((ns, LANES, 2 * HEAD_DIM), lambda h, j: (h, 0, 0))],
        out_shape=[jax.ShapeDtypeStruct((m, D_GLA), out_dtype),
                   jax.ShapeDtypeStruct((npair, LANES, 2 * HEAD_DIM), F32)],
        scratch_shapes=[pltpu.VMEM((ns, LANES, 2 * HEAD_DIM), F32)],
        compiler_params=_cparams(("arbitrary", "arbitrary")),
        name="gla_bwd" if reverse else "gla_fwd",
    )(*args)


def _gla_bidir(pf, pa, pb, cos, sin, wd2, bd2, gn, s0_f, s0_b):
    o_f, s_f = _gla_pass(pf, pa, pb, cos, sin, wd2[0], bd2[0], s0_f, reverse=False)
    out, s_b = _gla_pass(pf, pa, pb, cos, sin, wd2[1], bd2[1], s0_b, reverse=True, fin=(o_f, gn))
    return out, s_f, s_b


@functools.lru_cache(maxsize=None)
def _nat_table_consts():
    qcol = np.arange(GRID_W)
    cstart = np.clip(qcol - NAT_KC // 2, 0, GRID_W - NAT_KC)
    in_win = (qcol[None, :] >= cstart[:, None]) & (qcol[None, :] < cstart[:, None] + NAT_KC)
    col_idx = np.clip(qcol[None, :] - qcol[:, None] + NAT_KC - 1, 0, 2 * NAT_KC - 2)
    width = 2 * NAT_KC
    qc, half, kc = np.meshgrid(qcol, np.arange(2), qcol, indexing="ij")
    pos = (qc * 2 * GRID_W + half * GRID_W + kc).ravel()
    src = (half * width + col_idx[qc, kc]).ravel()
    valid = in_win[qc, kc].ravel()
    expand = np.zeros((2 * width, 2 * GRID_W * GRID_W), np.float32)
    expand[src[valid], pos[valid]] = 1.0
    neg = np.zeros((1, 2 * GRID_W * GRID_W), np.float32)
    neg[0, pos[~valid]] = NEG
    return expand, neg


def _nat_table_kernel(r_ref, e_ref, n_ref, o_ref):
    r = r_ref[...]
    hi = r.astype(BF16)
    rem = r - hi.astype(F32)
    mid = rem.astype(BF16)
    lo = (rem - mid.astype(F32)).astype(BF16)
    e = e_ref[...].astype(BF16)
    o_ref[...] = _dot(hi, e) + _dot(mid, e) + _dot(lo, e) + n_ref[...]


def _nat_bias_table(rpb_all):
    depth, nh = rpb_all.shape[0], rpb_all.shape[1]
    npair = 2 * NAT_KR - 2
    expand, neg = _nat_table_consts()
    rp = jnp.pad(rpb_all.astype(F32), ((0, 0), (0, 0), (0, 0), (0, 1)))
    rows = jnp.concatenate([rp[:, :, 0:npair], rp[:, :, 1:npair + 1]], axis=-1).reshape(depth * nh * npair, -1)
    ncol = expand.shape[1]
    bc = 2048
    out = pl.pallas_call(
        _nat_table_kernel,
        grid=(ncol // bc,),
        in_specs=[pl.BlockSpec(rows.shape, lambda i: (0, 0)),
                  pl.BlockSpec((expand.shape[0], bc), lambda i: (0, i)),
                  pl.BlockSpec((1, bc), lambda i: (0, i))],
        out_specs=pl.BlockSpec((rows.shape[0], bc), lambda i: (0, i)),
        out_shape=jax.ShapeDtypeStruct((rows.shape[0], ncol), F32),
        compiler_params=_cparams(("arbitrary",)),
        name="nat_table",
    )(rows, jnp.asarray(expand), jnp.asarray(neg))
    return out.reshape(depth, nh, npair, GRID_W, 2 * GRID_W)


def _nat_kernel(q_ref, k_ref, v_ref, kc_ref, vc_ref, tt_ref, o_ref, *, rb, rows):
    jb = pl.program_id(1)
    scale = HEAD_DIM ** -0.5
    nloc = NAT_KR * GRID_W
    grp = NAT_GROUP
    gq = grp * GRID_W

    def scores(g):
        q_all = q_ref[g * gq:(g + 1) * gq, :]
        s_ctx = _dot_nt(q_all, kc_ref[...]) * scale
        starts, s_rows = [], []
        for t in range(grp):
            r = jb * rb + g * grp + t
            rs = jnp.clip(r - NAT_KR // 2, 0, rows - NAT_KR)
            d = r - rs
            start = pl.multiple_of(rs * GRID_W, GRID_W)
            starts.append(start)
            kw = k_ref[pl.ds(start, nloc), :]
            bias = jnp.concatenate([tt_ref[0, 0, NAT_KR - 1 - d + 2 * p] for p in range(NAT_KR // 2)], axis=1)
            s_rows.append(_dot_nt(q_all[t * GRID_W:(t + 1) * GRID_W, :], kw) * scale + bias)
        return jnp.concatenate(s_rows, axis=0), s_ctx, starts

    def finish(g, s_loc, s_ctx, starts):
        mx = jnp.maximum(jnp.max(s_loc, axis=-1, keepdims=True), jnp.max(s_ctx, axis=-1, keepdims=True))
        e_loc = jnp.exp(s_loc - mx)
        e_ctx = jnp.exp(s_ctx - mx)
        den = jnp.sum(e_loc, axis=-1, keepdims=True) + jnp.sum(e_ctx, axis=-1, keepdims=True)
        e_loc = e_loc.astype(BF16)
        o_ctx = _dot(e_ctx.astype(BF16), vc_ref[...])
        o_rows = [_dot(e_loc[t * GRID_W:(t + 1) * GRID_W, :], v_ref[pl.ds(starts[t], nloc), :]) for t in range(grp)]
        o_ref[g * gq:(g + 1) * gq, :] = ((jnp.concatenate(o_rows, axis=0) + o_ctx) / den).astype(BF16)

    ngrp = rb // grp
    pending = scores(0)
    for g in range(ngrp):
        nxt = scores(g + 1) if g + 1 < ngrp else None
        finish(g, *pending)
        pending = nxt


NAT_GROUP = 8
NAT_ROWS = 32


def _nat(pb, pbc, tt, l):
    length = pb.shape[0]
    lc = pbc.shape[0]
    rows = length // GRID_W
    rb = min(NAT_ROWS, rows)
    npair = 2 * NAT_KR - 2
    return pl.pallas_call(
        functools.partial(_nat_kernel, rb=rb, rows=rows),
        grid=(N_NAT, rows // rb),
        in_specs=[pl.BlockSpec((rb * GRID_W, LANES), lambda h, j: (j, PB_NQ // LANES + h)),
                  pl.BlockSpec((length, LANES), lambda h, j: (0, PB_NK // LANES + h)),
                  pl.BlockSpec((length, LANES), lambda h, j: (0, PB_NV // LANES + h)),
                  pl.BlockSpec((lc, LANES), lambda h, j: (0, PB_NK // LANES + h)),
                  pl.BlockSpec((lc, LANES), lambda h, j: (0, PB_NV // LANES + h)),
                  pl.BlockSpec((1, 1, npair, GRID_W, 2 * GRID_W), lambda h, j: (l, h, 0, 0, 0))],
        out_specs=pl.BlockSpec((rb * GRID_W, LANES), lambda h, j: (j, h)),
        out_shape=jax.ShapeDtypeStruct((length, D_NAT), BF16),
        compiler_params=_cparams(("arbitrary", "arbitrary")),
        name="nat",
    )(pb, pb, pb, pbc, pbc, tt)


def _ctx_attn_kernel(q_ref, k_ref, v_ref, o_ref):
    s = _dot_nt(q_ref[...], k_ref[...]) * (HEAD_DIM ** -0.5)
    e = jnp.exp(s - jnp.max(s, axis=-1, keepdims=True))
    den = jnp.sum(e, axis=-1, keepdims=True)
    o_ref[...] = (_dot(e.astype(BF16), v_ref[...]) / den).astype(BF16)


def _ctx_attn(pbc):
    lc = pbc.shape[0]
    return pl.pallas_call(
        _ctx_attn_kernel,
        grid=(N_NAT,),
        in_specs=[pl.BlockSpec((lc, LANES), lambda h: (0, PB_NQ // LANES + h)),
                  pl.BlockSpec((lc, LANES), lambda h: (0, PB_NK // LANES + h)),
                  pl.BlockSpec((lc, LANES), lambda h: (0, PB_NV // LANES + h))],
        out_specs=pl.BlockSpec((lc, LANES), lambda h: (0, h)),
        out_shape=jax.ShapeDtypeStruct((lc, D_NAT), BF16),
        compiler_params=_cparams(("arbitrary",)),
        name="ctx_attn",
    )(pbc, pbc, pbc)


def _mm_post_kernel(*refs, splits, ni, n_total, ncast):
    na = len(splits)
    a_refs = refs[:na]
    w_ref, x_ref, gain_ref, gate_ref = refs[na:na + 4]
    cast_src = refs[na + 4:na + 4 + ncast]
    o_ref = refs[na + 4 + ncast]
    cast_dst = refs[na + 5 + ncast:na + 5 + 2 * ncast]
    acc_ref, ss_ref = refs[na + 5 + 2 * ncast:]
    _cast_plain(cast_src, cast_dst)
    i = pl.program_id(0)
    j = pl.program_id(1)
    slot = i % 2

    @pl.when(i < ni)
    def _():
        acc = None
        off = 0
        for a_ref, kk in zip(a_refs, splits):
            part = _dot(a_ref[...], w_ref[off:off + kk, :])
            acc = part if acc is None else acc + part
            off += kk
        acc_ref[slot, j] = acc
        sq = jnp.sum(acc * acc, axis=-1, keepdims=True)
        ss_ref[slot] = jnp.where(j == 0, sq, ss_ref[slot] + sq)

    @pl.when(i > 0)
    def _():
        prev = 1 - slot
        rstd = lax.rsqrt(ss_ref[prev] * (1.0 / n_total) + EPS)
        o_ref[...] = x_ref[...] + gate_ref[...] * (acc_ref[prev, j] * rstd * gain_ref[...])


def _mm_post(a_list, w, x, gain, gate, cast=()):
    m = x.shape[0]
    splits = tuple(a.shape[1] for a in a_list)
    k, n = w.shape
    bm = min(512, m)
    bn = 1024 if k <= 4096 else 512
    ni = m // bm
    nj = n // bn

    def prev_tile(i, j):
        return (jnp.maximum(i - 1, 0), jnp.where(i == 0, 0, j))

    c_in, c_out, c_shapes = _cast_specs(cast, [(s.shape[2],) for s, _ in cast], nj, (ni + 1) * nj)
    in_specs = [pl.BlockSpec((bm, kk), lambda i, j: (jnp.minimum(i, ni - 1), 0)) for kk in splits]
    in_specs += [pl.BlockSpec((k, bn), lambda i, j: (0, jnp.where(i == ni, nj - 1, j))),
                 pl.BlockSpec((bm, bn), prev_tile),
                 pl.BlockSpec((1, bn), lambda i, j: (0, j)),
                 pl.BlockSpec((1, bn), lambda i, j: (0, j))] + c_in
    res = pl.pallas_call(
        functools.partial(_mm_post_kernel, splits=splits, ni=ni, n_total=n, ncast=len(cast)),
        grid=(ni + 1, nj),
        in_specs=in_specs,
        out_specs=[pl.BlockSpec((bm, bn), prev_tile)] + c_out,
        out_shape=[jax.ShapeDtypeStruct((m, n), F32)] + c_shapes,
        scratch_shapes=[pltpu.VMEM((2, nj, bm, bn), F32), pltpu.VMEM((2, bm, 1), F32)],
        compiler_params=_cparams(("arbitrary", "arbitrary")),
        name="mm_post",
    )(*a_list, w, x, gain, gate, *[s for s, _ in cast])
    return res if cast else res[0]


FFN_HALO = BF16_SUBLANES // 2
FFN_BN = 512
FFN_PARTS = 1


def _ffn_up_kernel(*refs, bm, ni, ncast):
    x_ref, xp_ref, xn_ref, gain_ref, sc_ref, sh_ref, wg_ref, wv_ref, cg_ref, cv_ref = refs[:10]
    cast_src = refs[10:10 + ncast]
    o_ref = refs[10 + ncast]
    cast_dst = refs[11 + ncast:11 + 2 * ncast]
    h_ref = refs[11 + 2 * ncast]
    _cast_plain(cast_src, cast_dst)
    i = pl.program_id(0)
    j = pl.program_id(1)
    hl = FFN_HALO

    @pl.when(j == 0)
    def _():
        h_ref[0:bm, :] = _modulate(x_ref[...], gain_ref[...], sc_ref[...], sh_ref[...]).astype(BF16)

    @pl.when(j == 0)
    def _():
        gain, sc, sh = gain_ref[...], sc_ref[...], sh_ref[...]
        hp = _modulate(xp_ref[...], gain, sc, sh)
        hn = _modulate(xn_ref[...], gain, sc, sh)
        halo = jnp.concatenate([jnp.where(i > 0, hp, 0.0), jnp.where(i < ni - 1, hn, 0.0)], axis=0)
        h_ref[bm:bm + 2 * hl, :] = halo.astype(BF16)

    h = h_ref[...]
    part = o_ref.shape[1] // FFN_PARTS
    row = lax.broadcasted_iota(jnp.int32, (bm, part), 0)

    def conv(w_ref, cw_ref, cols):
        y = _dot(h, w_ref[:, cols])
        ym = y[0:bm, :]
        y_prev = jnp.where(row == 0, y[bm + hl - 1:bm + hl, :], pltpu.roll(ym, 1, 0))
        y_next = jnp.where(row == bm - 1, y[bm + hl:bm + hl + 1, :], pltpu.roll(ym, bm - 1, 0))
        return y_prev * cw_ref[0:1, cols] + ym * cw_ref[1:2, cols] + y_next * cw_ref[2:3, cols]

    for p in range(FFN_PARTS):
        cols = slice(p * part, (p + 1) * part)
        gate = conv(wg_ref, cg_ref, cols)
        val = conv(wv_ref, cv_ref, cols)
        o_ref[:, cols] = (gate * jax.nn.sigmoid(gate) * val).astype(BF16)


def _ffn_up(x, gain, scale, shift, w, cw, cast=()):
    m, d = x.shape
    bm = min(512, m)
    ni = m // bm
    bn = FFN_BN
    nt = D_FF // bn
    hl = FFN_HALO
    per = bm // hl
    nhalo = m // hl
    vec = pl.BlockSpec((1, d), lambda i, j: (0, 0))
    c_in, c_out, c_shapes = _cast_specs(cast, [(s.shape[2],) for s, _ in cast], nt, ni * nt)
    res = pl.pallas_call(
        functools.partial(_ffn_up_kernel, bm=bm, ni=ni, ncast=len(cast)),
        grid=(ni, nt),
        in_specs=[pl.BlockSpec((bm, d), lambda i, j: (i, 0)),
                  pl.BlockSpec((hl, d), lambda i, j: (jnp.maximum(i * per - 1, 0), 0)),
                  pl.BlockSpec((hl, d), lambda i, j: (jnp.minimum((i + 1) * per, nhalo - 1), 0)),
                  vec, vec, vec,
                  pl.BlockSpec((d, bn), lambda i, j: (0, j)),
                  pl.BlockSpec((d, bn), lambda i, j: (0, nt + j)),
                  pl.BlockSpec((3, bn), lambda i, j: (0, j)),
                  pl.BlockSpec((3, bn), lambda i, j: (0, nt + j))] + c_in,
        out_specs=[pl.BlockSpec((bm, bn), lambda i, j: (i, j))] + c_out,
        out_shape=[jax.ShapeDtypeStruct((m, D_FF), BF16)] + c_shapes,
        scratch_shapes=[pltpu.VMEM((bm + 2 * hl, d), BF16)],
        compiler_params=_cparams(("arbitrary", "arbitrary")),
        name="ffn_up",
    )(x, x, x, gain, scale, shift, w, w, cw, cw, *[s for s, _ in cast])
    return res if cast else res[0]


def _prep_w_in_rows(wt_l):
    return jnp.concatenate([wt_l[W_IN_OFFS[t]:W_IN_OFFS[t + 1]] for t in W_IN_ORDER], axis=0).astype(BF16)


def _prep_decay(w_dec, b_dec):
    wd = jnp.zeros((2, LANES, D_GLA_K), F32)
    wd = wd.at[0, 0:GLA_RANK].set(w_dec[0]).at[1, GLA_RANK:2 * GLA_RANK].set(w_dec[1])
    return wd, b_dec.reshape(2, 1, D_GLA_K)


def _state_zero():
    return jnp.zeros((N_GLA // 2, LANES, 2 * HEAD_DIM), F32)


def kernel(x, c, ctx, c_ctx, ada_w, ada_b, norm_mix_pre, norm_mix_post, w_in, gla_w_decay, gla_b_decay,
           gla_norm, fourier_w, nat_rpb, w_out, norm_ffn_pre, norm_ffn_post, ffn_w_up, ffn_w_conv, ffn_w_down):
    depth = ada_w.shape[0]
    d = x.shape[-1]
    xs = x[0]
    cs = ctx[0]
    seq = xs.shape[0]
    lc = cs.shape[0]

    mod = _ada_mod(c, c_ctx, ada_w, ada_b)
    cos_x, sin_x = _rope_tables(seq, True)
    cos_c, sin_c = _rope_tables(lc, False)

    w_in_t = jnp.swapaxes(w_in, 1, 2)
    w_at = w_in_t
    w_cat = _prep_w_in_rows(w_in_t[0])
    side_w_in = (seq // min(512, seq)) * ((N_PB + N_PF) // IN_BN) >= (N_PB + N_PF) // W_IN_CAST_ROWS
    w_out_b = w_out[0].astype(BF16)
    w_up_b = ffn_w_up[0].astype(BF16)
    w_down_b = ffn_w_down[0].astype(BF16)
    tt = _nat_bias_table(nat_rpb)

    for l in range(depth):
        last = l == depth - 1
        mx = [mod[l, 0:1, t * d:(t + 1) * d] for t in range(N_MOD)]
        mc = [mod[l, 1:2, t * d:(t + 1) * d] for t in range(N_MOD)]
        vrow = lambda t: t[l].reshape(1, -1)

        wd2, bd2 = _prep_decay(gla_w_decay[l], gla_b_decay[l])
        wc = _four_weights(fourier_w[l])
        gn = vrow(gla_norm)

        res = _inproj(xs, vrow(norm_mix_pre), mx[1], mx[0], w_cat, w_at, l,
                      cast_src=None if (last or not side_w_in) else (w_in_t, l + 1))
        pbx, pfx, pax = res[:3]
        pbc, pfc, pac = _inproj(cs, vrow(norm_mix_pre), mc[1], mc[0], w_cat, w_at, l)

        gla_c, sc_f, sc_b = _gla_bidir(pfc, pac, pbc, cos_c, sin_c, wd2, bd2, gn, _state_zero(), _state_zero())
        gla_x, _, _ = _gla_bidir(pfx, pax, pbx, cos_x, sin_x, wd2, bd2, gn, sc_f, sc_b)
        four_x = _fourier_mix(pbx, wc)
        nat_x = _nat(pbx, pbc, tt, l)
        xs = _mm_post([four_x, gla_x, nat_x], w_out_b, xs, vrow(norm_mix_post), mx[2])
        if not last:
            four_c = _fourier_mix(pbc, wc)
            nat_c = _ctx_attn(pbc)
            cs = _mm_post([four_c, gla_c, nat_c], w_out_b, cs, vrow(norm_mix_post), mc[2])

        if last:
            act_x = _ffn_up(xs, vrow(norm_ffn_pre), mx[4], mx[3], w_up_b, ffn_w_conv[l])
            xs = _mm_post([act_x], w_down_b, xs, vrow(norm_ffn_post), mx[5])
        else:
            act_x, w_down_n, w_out_n = _ffn_up(xs, vrow(norm_ffn_pre), mx[4], mx[3], w_up_b, ffn_w_conv[l],
                                               cast=((ffn_w_down, l + 1), (w_out, l + 1)))
            xs, w_up_n = _mm_post([act_x], w_down_b, xs, vrow(norm_ffn_post), mx[5], cast=((ffn_w_up, l + 1),))
            act_c = _ffn_up(cs, vrow(norm_ffn_pre), mc[4], mc[3], w_up_b, ffn_w_conv[l])
            cs = _mm_post([act_c], w_down_b, cs, vrow(norm_ffn_post), mc[5])
            w_cat = res[3] if side_w_in else _prep_w_in_rows(w_in_t[l + 1])
            w_out_b, w_up_b, w_down_b = w_out_n, w_up_n, w_down_n
    return xs[None]
```

```python
import functools

import numpy as np
import jax
import jax.numpy as jnp
from jax import lax
from jax.experimental import pallas as pl
from jax.experimental.pallas import tpu as pltpu

F32 = jnp.float32
BF16 = jnp.bfloat16

D_MODEL = 4096
GRID_W = 64
HEAD_DIM = 128
N_FOURIER = 8
N_GLA = 12
N_NAT = 12
D_FOURIER = N_FOURIER * HEAD_DIM
D_GLA = N_GLA * HEAD_DIM
GLA_DK = HEAD_DIM // 2
D_GLA_K = N_GLA * GLA_DK
GLA_RANK = 16
GLA_TAU = 16.0
GLA_CHUNK = 64
D_NAT = N_NAT * HEAD_DIM
NAT_KR = 8
NAT_KC = 16
ROPE_BASE = 10000.0
D_FF = 11 * D_MODEL // 8
N_MOD = 6
EPS = 1e-6
NEG = -1e30

LANES = 128
BF16_SUBLANES = 16
VMEM_LIMIT_MB = 56

PB_V, PB_NQ, PB_NK, PB_NV, PB_F = 0, 1536, 3072, 4608, 6144
N_PB = 7168
PF_Q, PF_K, PF_G = 0, 768, 1536
N_PF = 3072
IN_BN = 1024
GLA_SLABS = 2


def _cparams(sem, flags=None):
    return pltpu.CompilerParams(dimension_semantics=sem, vmem_limit_bytes=VMEM_LIMIT_MB * 1024 * 1024, flags=flags)


def _dot(a, b):
    return jnp.dot(a, b, preferred_element_type=F32)


def _dot_nt(a, b):
    return lax.dot_general(a, b, (((1,), (1,)), ((), ())), preferred_element_type=F32)


def _dot_tn(a, b):
    return lax.dot_general(a, b, (((0,), (0,)), ((), ())), preferred_element_type=F32)


def _split_bf16(x):
    hi = x.astype(BF16)
    lo = (x - hi.astype(F32)).astype(BF16)
    return hi, lo


def _ada_kernel(cc_ref, w_ref, b_ref, o_ref):
    s = cc_ref[...]
    s = s * jax.nn.sigmoid(s)
    s_hi = s.astype(BF16).astype(F32)
    row = lax.broadcasted_iota(jnp.int32, s.shape, 0)
    lhs = jnp.where(row < 8, s_hi, s - s_hi).astype(BF16)
    w_hi, w_lo = _split_bf16(w_ref[0])
    r = _dot(lhs, w_hi) + _dot(lhs, w_lo)
    o_ref[0] = r[0:8] + r[8:16] + b_ref[0]


def _ada_mod(c, c_ctx, ada_w, ada_b):
    depth, d, n = ada_w.shape
    bn = 512
    cc = jnp.zeros((16, d), F32)
    cc = cc.at[0].set(c[0]).at[1].set(c_ctx).at[8].set(c[0]).at[9].set(c_ctx)
    return pl.pallas_call(
        _ada_kernel,
        grid=(depth, n // bn),
        in_specs=[pl.BlockSpec((16, d), lambda l, j: (0, 0)),
                  pl.BlockSpec((1, d, bn), lambda l, j: (l, 0, j)),
                  pl.BlockSpec((1, 1, bn), lambda l, j: (l, 0, j))],
        out_specs=pl.BlockSpec((1, 8, bn), lambda l, j: (l, 0, j)),
        out_shape=jax.ShapeDtypeStruct((depth, 8, n), F32),
        compiler_params=_cparams(("arbitrary", "arbitrary")),
        name="ada_mod",
    )(cc, ada_w, ada_b.reshape(depth, 1, n))


def _modulate(x, gain, scale, shift):
    ms = jnp.mean(x * x, axis=-1, keepdims=True)
    return x * lax.rsqrt(ms + EPS) * (gain * (1.0 + scale)) + shift


CAST_ROWS = BF16_SUBLANES


def _cast_specs(srcs, out_widths, nj, nsteps):
    in_specs, out_specs, out_shapes = [], [], []
    for (s, l), widths in zip(srcs, out_widths):
        nrow, ncol = s.shape[1], s.shape[2]
        rows = next(r for r in range(CAST_ROWS, nrow + 1, CAST_ROWS) if nrow % r == 0 and nrow // r <= nsteps)
        nchunk = nrow // rows

        def chunk(*g, nchunk=nchunk):
            step = g[0] if nj is None else g[0] * nj + g[1]
            return jnp.minimum(step, nchunk - 1)

        in_specs.append(pl.BlockSpec((1, rows, ncol), lambda *g, l=l, chunk=chunk: (l, chunk(*g), 0)))
        for wd in widths:
            out_specs.append(pl.BlockSpec((rows, wd), lambda *g, chunk=chunk: (chunk(*g), 0)))
            out_shapes.append(jax.ShapeDtypeStruct((nrow, wd), BF16))
    return in_specs, out_specs, out_shapes


def _cast_plain(src_refs, dst_refs):
    for s_ref, d_ref in zip(src_refs, dst_refs):
        d_ref[...] = s_ref[0].astype(BF16)


W_IN_SIZES = (D_FOURIER, 2 * D_GLA_K, D_GLA, D_GLA, 2 * GLA_RANK, 3 * D_NAT)
W_IN_OFFS = tuple(int(t) for t in np.concatenate([[0], np.cumsum(W_IN_SIZES)]))
W_IN_ORDER = (2, 5, 0, 1, 3)
W_IN_CAST_ROWS = 32


def _w_in_src_block(c):
    r = W_IN_CAST_ROWS
    src = None
    dst_lo = 0
    for t in W_IN_ORDER:
        n = W_IN_SIZES[t] // r
        here = W_IN_OFFS[t] // r + (c - dst_lo)
        src = here if src is None else jnp.where(c >= dst_lo, here, src)
        dst_lo += n
    return src


def _inproj_kernel(*refs, nb_tiles, cast):
    if cast:
        x_ref, gain_ref, sc_ref, sh_ref, w_ref, wa_ref, src_ref, ob_ref, of_ref, oa_ref, dst_ref, h_ref = refs
        dst_ref[...] = src_ref[0].astype(BF16)
    else:
        x_ref, gain_ref, sc_ref, sh_ref, w_ref, wa_ref, ob_ref, of_ref, oa_ref, h_ref = refs
    j = pl.program_id(1)

    @pl.when(j == 0)
    def _():
        h = _modulate(x_ref[...], gain_ref[...], sc_ref[...], sh_ref[...]).astype(BF16)
        h_ref[...] = h
        wa = wa_ref[0].astype(BF16)
        wa = jnp.concatenate([wa, jnp.zeros((LANES - wa.shape[0], wa.shape[1]), BF16)], axis=0)
        oa_ref[...] = _dot_nt(h, wa)

    r = _dot_nt(h_ref[...], w_ref[...])

    @pl.when(j < nb_tiles)
    def _():
        ob_ref[...] = r.astype(BF16)

    @pl.when(j >= nb_tiles)
    def _():
        of_ref[...] = r


def _inproj(x, gain, scale, shift, wt, wat, l, cast_src=None):
    m, d = x.shape
    bm = min(512, m)
    bn = IN_BN
    nb_tiles = N_PB // bn
    nf_tiles = N_PF // bn
    nj = nb_tiles + nf_tiles
    nrow = N_PB + N_PF
    vec = pl.BlockSpec((1, d), lambda i, j: (0, 0))
    in_specs = [pl.BlockSpec((bm, d), lambda i, j: (i, 0)), vec, vec, vec,
                pl.BlockSpec((bn, d), lambda i, j: (j, 0)),
                pl.BlockSpec((1, W_IN_SIZES[4], d), lambda i, j: (l, W_IN_OFFS[4] // W_IN_SIZES[4], 0))]
    out_specs = [pl.BlockSpec((bm, bn), lambda i, j: (i, jnp.minimum(j, nb_tiles - 1))),
                 pl.BlockSpec((bm, bn), lambda i, j: (i, jnp.maximum(j - nb_tiles, 0))),
                 pl.BlockSpec((bm, LANES), lambda i, j: (i, 0))]
    out_shape = [jax.ShapeDtypeStruct((m, N_PB), BF16), jax.ShapeDtypeStruct((m, N_PF), F32),
                 jax.ShapeDtypeStruct((m, LANES), F32)]
    args = [x, gain, scale, shift, wt, wat]
    if cast_src is not None:
        src, ls = cast_src
        nchunk = nrow // W_IN_CAST_ROWS
        assert nchunk <= (m // bm) * nj

        def chunk(i, j):
            return jnp.minimum(i * nj + j, nchunk - 1)

        in_specs.append(pl.BlockSpec((1, W_IN_CAST_ROWS, d), lambda i, j: (ls, _w_in_src_block(chunk(i, j)), 0)))
        out_specs.append(pl.BlockSpec((W_IN_CAST_ROWS, d), lambda i, j: (chunk(i, j), 0)))
        out_shape.append(jax.ShapeDtypeStruct((nrow, d), BF16))
        args.append(src)
    return pl.pallas_call(
        functools.partial(_inproj_kernel, nb_tiles=nb_tiles, cast=cast_src is not None),
        grid=(m // bm, nj),
        in_specs=in_specs,
        out_specs=out_specs,
        out_shape=out_shape,
        scratch_shapes=[pltpu.VMEM((bm, d), BF16)],
        compiler_params=_cparams(("arbitrary", "arbitrary")),
        name="inproj",
    )(*args)


def _four_w_kernel(cc_ref, sc_ref, fw_ref, o_ref):
    c_hi, c_lo = _split_bf16(cc_ref[...])
    s_hi, s_lo = _split_bf16(sc_ref[...])
    for g in range(N_FOURIER):
        w_hi, w_lo = _split_bf16(fw_ref[g])
        a = _dot(c_hi, w_hi) + _dot(c_hi, w_lo) + _dot(c_lo, w_hi)
        b = _dot(s_hi, w_hi) + _dot(s_hi, w_lo) + _dot(s_lo, w_hi)
        o_ref[g, :, 0:HEAD_DIM] = a
        o_ref[g, :, HEAD_DIM:2 * HEAD_DIM] = -b


def _four_weights(fourier_w_l):
    k = np.arange(HEAD_DIM)
    ang = 2.0 * np.pi * ((k[:, None] * k[None, :]) % HEAD_DIM) / HEAD_DIM
    cc = jnp.asarray(np.cos(ang), F32)
    sc = jnp.asarray(np.sin(ang), F32)
    return pl.pallas_call(
        _four_w_kernel,
        out_shape=jax.ShapeDtypeStruct((N_FOURIER, HEAD_DIM, 2 * HEAD_DIM), F32),
        name="four_w",
    )(cc, sc, fourier_w_l)


def _four_s0_kernel(u_ref, wc_ref, v_ref, *, norm):
    for g in range(N_FOURIER):
        cols = slice(g * HEAD_DIM, (g + 1) * HEAD_DIM)
        y = _dot(u_ref[:, cols], wc_ref[g].astype(BF16)) * norm
        v_ref[0, :, cols] = y[:, 0:HEAD_DIM].astype(BF16)
        v_ref[1, :, cols] = y[:, HEAD_DIM:2 * HEAD_DIM].astype(BF16)


def _four_s1_kernel(m1_ref, v_ref, z_ref):
    z_ref[...] = _dot(m1_ref[...].astype(BF16), v_ref[...]).astype(BF16)


def _four_s2_kernel(t_ref, z_ref, o_ref):
    kb, n2, ch = z_ref.shape[1], z_ref.shape[2], z_ref.shape[3]
    for kk in range(kb):
        z = jnp.concatenate([z_ref[0, kk], z_ref[1, kk]], axis=0)
        o_ref[:, kk * ch:(kk + 1) * ch] = _dot(t_ref[kk].astype(BF16), z).astype(BF16)


@functools.lru_cache(maxsize=None)
def _four_tables(n1, n2):
    length = n1 * n2
    m1 = None
    if n1 > 1:
        j = np.arange(n1)
        ang = 2.0 * np.pi * ((j[:, None] * j[None, :]) % n1) / n1
        cs, ss = np.cos(ang), np.sin(ang)
        m1 = np.block([[cs, ss], [-ss, cs]]).astype(np.float32)
    k1 = np.arange(n1, dtype=np.int64)[:, None, None]
    k2 = np.arange(n2, dtype=np.int64)[None, :, None]
    j2 = np.arange(n2, dtype=np.int64)[None, None, :]
    ang = 2.0 * np.pi * ((j2 * (k1 + n1 * k2)) % length) / length
    t2 = np.concatenate([np.cos(ang), np.sin(ang)], axis=-1).astype(np.float32)
    return m1, t2


def _fourier_mix(pb, wc):
    length = pb.shape[0]
    ch = D_FOURIER
    n2 = length if length <= 512 else 128
    n1 = length // n2
    m1, t2 = _four_tables(n1, n2)
    norm = float(1.0 / np.sqrt(float(length) * HEAD_DIM))
    bm = min(512, length)
    v = pl.pallas_call(
        functools.partial(_four_s0_kernel, norm=norm),
        grid=(length // bm,),
        in_specs=[pl.BlockSpec((bm, ch), lambda i: (i, PB_F // ch)),
                  pl.BlockSpec((N_FOURIER, HEAD_DIM, 2 * HEAD_DIM), lambda i: (0, 0, 0))],
        out_specs=pl.BlockSpec((2, bm, ch), lambda i: (0, i, 0)),
        out_shape=jax.ShapeDtypeStruct((2, length, ch), BF16),
        compiler_params=_cparams(("arbitrary",)),
        name="four_s0",
    )(pb, wc)
    if n1 > 1:
        ncol = n2 * ch
        bc = min(4096, ncol)
        z = pl.pallas_call(
            _four_s1_kernel,
            grid=(ncol // bc,),
            in_specs=[pl.BlockSpec((2 * n1, 2 * n1), lambda i: (0, 0)),
                      pl.BlockSpec((2 * n1, bc), lambda i: (0, i))],
            out_specs=pl.BlockSpec((2 * n1, bc), lambda i: (0, i)),
            out_shape=jax.ShapeDtypeStruct((2 * n1, ncol), BF16),
            compiler_params=_cparams(("arbitrary",)),
            name="four_s1",
        )(jnp.asarray(m1), v.reshape(2 * n1, ncol))
    else:
        z = v
    z4 = z.reshape(2, n1, n2, ch)
    kb = min(4, n1)
    out = pl.pallas_call(
        _four_s2_kernel,
        grid=(n1 // kb,),
        in_specs=[pl.BlockSpec((kb, n2, 2 * n2), lambda i: (i, 0, 0)),
                  pl.BlockSpec((2, kb, n2, ch), lambda i: (0, i, 0, 0))],
        out_specs=pl.BlockSpec((n2, kb * ch), lambda i: (0, i)),
        out_shape=jax.ShapeDtypeStruct((n2, n1 * ch), BF16),
        compiler_params=_cparams(("arbitrary",)),
        name="four_s2",
    )(jnp.asarray(t2), z4)
    return out.reshape(length, ch)


def _rope_tables(length, use_rope):
    if not use_rope:
        return jnp.ones((length, LANES), F32), jnp.zeros((length, LANES), F32)
    rows = length // GRID_W
    half = GLA_DK // 4
    freqs = ROPE_BASE ** (-jnp.arange(half, dtype=F32) / half)
    ar = jnp.arange(rows, dtype=F32)[:, None] * freqs
    ac = jnp.arange(GRID_W, dtype=F32)[:, None] * freqs

    def table(fr, fc, sign):
        r = jnp.broadcast_to(fr(ar)[:, None, :], (rows, GRID_W, half))
        c = jnp.broadcast_to(fc(ac)[None, :, :], (rows, GRID_W, half))
        t64 = jnp.concatenate([sign * r, r, sign * c, c], axis=-1).reshape(length, GLA_DK)
        return jnp.concatenate([t64, t64], axis=-1)

    return table(jnp.cos, jnp.cos, 1.0), table(jnp.sin, jnp.sin, -1.0)


def _rope(x, cos, sin):
    lane = lax.broadcasted_iota(jnp.int32, x.shape, 1)
    first = (lane % 32) < 16
    partner = jnp.where(first, pltpu.roll(x, LANES - 16, 1), pltpu.roll(x, 16, 1))
    return x * cos + partner * sin


def _gla_kernel(*refs, reverse, nchunk, nblk, finalize):
    if finalize:
        (q_ref, k_ref, a_ref, v_ref, cos_ref, sin_ref, wd_ref, bd_ref, tri_ref, s0_ref,
         of_ref, g_ref, gn_ref, o_ref, sfin_ref, st_ref) = refs
    else:
        (q_ref, k_ref, a_ref, v_ref, cos_ref, sin_ref, wd_ref, bd_ref, tri_ref, s0_ref,
         o_ref, sfin_ref, st_ref) = refs
        of_ref = g_ref = gn_ref = None
    jb = pl.program_id(1)

    @pl.when(jb == 0)
    def _():
        st_ref[...] = s0_ref[...]

    c = GLA_CHUNK
    slabs = range(GLA_SLABS)
    prep = []
    for sl in slabs:
        one = slice(sl * LANES, (sl + 1) * LANES)
        prep.append(_gla_prep(q_ref[:, one], k_ref[:, one], a_ref[...], cos_ref[...], sin_ref[...],
                              wd_ref[:, one], bd_ref[:, one], tri_ref[...], reverse=reverse, nchunk=nchunk))
    masks = _gla_masks(reverse)
    states = [st_ref[sl] for sl in slabs]
    order = range(nchunk - 1, -1, -1) if reverse else range(nchunk)
    for ci in order:
        for sl in slabs:
            q_dec, k_inv, dec_cols = prep[sl]
            rows = slice(ci * c, (ci + 1) * c)
            two = slice(2 * sl * LANES, 2 * (sl + 1) * LANES)
            states[sl] = _gla_chunk(q_dec[rows, :], k_inv[rows, :], dec_cols[:, ci:ci + 1], states[sl], masks,
                                    v_ref, of_ref, g_ref, gn_ref, o_ref, rows, two, finalize=finalize)
    for sl in slabs:
        st_ref[sl] = states[sl]

    @pl.when(jb == nblk - 1)
    def _():
        for sl in slabs:
            sfin_ref[sl] = states[sl]


def _gla_prep(q, k, a, cos, sin, wd, bd, tri, *, reverse, nchunk):
    c = GLA_CHUNK
    z = _dot(a.astype(BF16), wd.astype(BF16)) + bd
    g = (jnp.minimum(z, 0.0) - jnp.log(1.0 + jnp.exp(-jnp.abs(z)))) * (1.0 / GLA_TAU)
    g_hi, g_lo = _split_bf16(g)
    bsum = _dot(tri, jnp.concatenate([g_hi, g_lo], axis=1))
    b = bsum[:, 0:LANES] + bsum[:, LANES:2 * LANES]
    q_dec = _rope(q, cos, sin) * (GLA_DK ** -0.5) * jnp.exp(b)
    k_inv = _rope(k, cos, sin) * jnp.exp(-b)
    lasts = [(ci * c if reverse else ci * c + c - 1) for ci in range(nchunk)]
    b_last = jnp.concatenate([b[r:r + 1, :] for r in lasts] +
                             [jnp.zeros((8 - nchunk % 8, LANES), F32)] * (1 if nchunk % 8 else 0), axis=0)
    dec_cols = jnp.exp(b_last.T)
    return q_dec, k_inv, dec_cols


def _gla_masks(reverse):
    c = GLA_CHUNK
    lane = lax.broadcasted_iota(jnp.int32, (c, LANES), 1)
    row = lax.broadcasted_iota(jnp.int32, (c, LANES), 0)
    head0 = lane < GLA_DK
    key_pos = lane % c
    causal = (key_pos >= row) if reverse else (key_pos <= row)
    vhead0 = lax.broadcasted_iota(jnp.int32, (c, 2 * LANES), 1) < LANES
    srow = lax.broadcasted_iota(jnp.int32, (LANES, 2 * HEAD_DIM), 0)
    slane = lax.broadcasted_iota(jnp.int32, (LANES, 2 * HEAD_DIM), 1)
    state_mask = (srow < GLA_DK) == (slane < HEAD_DIM)
    return head0, causal, vhead0, state_mask


def _gla_chunk(q_dec, ki, dec_col, s, masks, v_ref, of_ref, g_ref, gn_ref, o_ref, rows, two, *, finalize):
    head0, causal, vhead0, state_mask = masks
    qd = q_dec.astype(BF16)
    zero = jnp.zeros_like(ki)
    kbd = jnp.concatenate([jnp.where(head0, ki, zero), jnp.where(head0, zero, ki)], axis=0).astype(BF16)
    att = _dot_nt(qd, kbd)
    att = jnp.where(causal, att, 0.0).astype(BF16)
    vc = v_ref[rows, two]
    vzero = jnp.zeros_like(vc)
    vbd = jnp.concatenate([jnp.where(vhead0, vc, vzero), jnp.where(vhead0, vzero, vc)], axis=0)
    o = _dot(att, vbd) + _dot(qd, s.astype(BF16))
    if finalize:
        o = o + of_ref[rows, two]
        gate = g_ref[rows, two]
        gate = gate * jax.nn.sigmoid(gate)
        gn = gn_ref[...]
        outs = []
        for h in range(2):
            oh = o[:, h * HEAD_DIM:(h + 1) * HEAD_DIM]
            ms = jnp.mean(oh * oh, axis=-1, keepdims=True)
            outs.append(oh * lax.rsqrt(ms + EPS) * gn)
        o_ref[rows, two] = (jnp.concatenate(outs, axis=1) * gate).astype(o_ref.dtype)
    else:
        o_ref[rows, two] = o
    u = _dot_tn(ki.astype(BF16), vc)
    return dec_col * (s + jnp.where(state_mask, u, 0.0))


def _gla_tri(tb, reverse):
    t = np.arange(tb)
    same = (t[:, None] // GLA_CHUNK) == (t[None, :] // GLA_CHUNK)
    tri = (t[None, :] >= t[:, None]) if reverse else (t[None, :] <= t[:, None])
    return jnp.asarray((same & tri).astype(np.float32), BF16)


def _gla_pass(pf, pa, pb, cos, sin, wd, bd, s0, reverse, fin=None):
    m = pf.shape[0]
    tb = min(1024, m)
    nblk = m // tb
    npair = N_GLA // 2
    finalize = fin is not None

    def rowblk(j):
        return (nblk - 1 - j) if reverse else j

    ns = GLA_SLABS
    w1 = ns * LANES
    w2 = 2 * w1
    in_specs = [
        pl.BlockSpec((tb, w1), lambda h, j: (rowblk(j), PF_Q // w1 + h)),
        pl.BlockSpec((tb, w1), lambda h, j: (rowblk(j), PF_K // w1 + h)),
        pl.BlockSpec((tb, LANES), lambda h, j: (rowblk(j), 0)),
        pl.BlockSpec((tb, w2), lambda h, j: (rowblk(j), PB_V // w2 + h)),
        pl.BlockSpec((tb, LANES), lambda h, j: (rowblk(j), 0)),
        pl.BlockSpec((tb, LANES), lambda h, j: (rowblk(j), 0)),
        pl.BlockSpec((LANES, w1), lambda h, j: (0, h)),
        pl.BlockSpec((1, w1), lambda h, j: (0, h)),
        pl.BlockSpec((tb, tb), lambda h, j: (0, 0)),
        pl.BlockSpec((ns, LANES, 2 * HEAD_DIM), lambda h, j: (h, 0, 0)),
    ]
    args = [pf, pf, pa, pb, cos, sin, wd, bd, _gla_tri(tb, reverse), s0]
    if finalize:
        o_other, gn = fin
        in_specs += [
            pl.BlockSpec((tb, w2), lambda h, j: (rowblk(j), h)),
            pl.BlockSpec((tb, w2), lambda h, j: (rowblk(j), PF_G // w2 + h)),
            pl.BlockSpec((1, HEAD_DIM), lambda h, j: (0, 0)),
        ]
        args += [o_other, pf, gn]
    out_dtype = BF16 if finalize else F32
    return pl.pallas_call(
        functools.partial(_gla_kernel, reverse=reverse, nchunk=tb // GLA_CHUNK, nblk=nblk, finalize=finalize),
        grid=(npair // ns, nblk),
        in_specs=in_specs,
        out_specs=[pl.BlockSpec((tb, w2), lambda h, j: (rowblk(j), h)),
                   pl.BlockSpec((ns, LANES, 2 * HEAD_DIM), lambda h, j: (h, 0, 0))],
        out_shape=[jax.ShapeDtypeStruct((m, D_GLA), out_dtype),
                   jax.ShapeDtypeStruct((npair, LANES, 2 * HEAD_DIM), F32)],
        scratch_shapes=[pltpu.VMEM((ns, LANES, 2 * HEAD_DIM), F32)],
        compiler_params=_cparams(("arbitrary", "arbitrary")),
        name="gla_bwd" if reverse else "gla_fwd",
    )(*args)


def _gla_bidir(pf, pa, pb, cos, sin, wd2, bd2, gn, s0_f, s0_b):
    o_f, s_f = _gla_pass(pf, pa, pb, cos, sin, wd2[0], bd2[0], s0_f, reverse=False)
    out, s_b = _gla_pass(pf, pa, pb, cos, sin, wd2[1], bd2[1], s0_b, reverse=True, fin=(o_f, gn))
    return out, s_f, s_b


@functools.lru_cache(maxsize=None)
def _nat_table_consts():
    qcol = np.arange(GRID_W)
    cstart = np.clip(qcol - NAT_KC // 2, 0, GRID_W - NAT_KC)
    in_win = (qcol[None, :] >= cstart[:, None]) & (qcol[None, :] < cstart[:, None] + NAT_KC)
    col_idx = np.clip(qcol[None, :] - qcol[:, None] + NAT_KC - 1, 0, 2 * NAT_KC - 2)
    width = 2 * NAT_KC
    qc, half, kc = np.meshgrid(qcol, np.arange(2), qcol, indexing="ij")
    pos = (qc * 2 * GRID_W + half * GRID_W + kc).ravel()
    src = (half * width + col_idx[qc, kc]).ravel()
    valid = in_win[qc, kc].ravel()
    expand = np.zeros((2 * width, 2 * GRID_W * GRID_W), np.float32)
    expand[src[valid], pos[valid]] = 1.0
    neg = np.zeros((1, 2 * GRID_W * GRID_W), np.float32)
    neg[0, pos[~valid]] = NEG
    return expand, neg


def _nat_table_kernel(r_ref, e_ref, n_ref, o_ref):
    r = r_ref[...]
    hi = r.astype(BF16)
    rem = r - hi.astype(F32)
    mid = rem.astype(BF16)
    lo = (rem - mid.astype(F32)).astype(BF16)
    e = e_ref[...].astype(BF16)
    o_ref[...] = _dot(hi, e) + _dot(mid, e) + _dot(lo, e) + n_ref[...]


def _nat_bias_table(rpb_all):
    depth, nh = rpb_all.shape[0], rpb_all.shape[1]
    npair = 2 * NAT_KR - 2
    expand, neg = _nat_table_consts()
    rp = jnp.pad(rpb_all.astype(F32), ((0, 0), (0, 0), (0, 0), (0, 1)))
    rows = jnp.concatenate([rp[:, :, 0:npair], rp[:, :, 1:npair + 1]], axis=-1).reshape(depth * nh * npair, -1)
    ncol = expand.shape[1]
    bc = 2048
    out = pl.pallas_call(
        _nat_table_kernel,
        grid=(ncol // bc,),
        in_specs=[pl.BlockSpec(rows.shape, lambda i: (0, 0)),
                  pl.BlockSpec((expand.shape[0], bc), lambda i: (0, i)),
                  pl.BlockSpec((1, bc), lambda i: (0, i))],
        out_specs=pl.BlockSpec((rows.shape[0], bc), lambda i: (0, i)),
        out_shape=jax.ShapeDtypeStruct((rows.shape[0], ncol), F32),
        compiler_params=_cparams(("arbitrary",)),
        name="nat_table",
    )(rows, jnp.asarray(expand), jnp.asarray(neg))
    return out.reshape(depth, nh, npair, GRID_W, 2 * GRID_W)


def _nat_kernel(q_ref, k_ref, v_ref, kc_ref, vc_ref, tt_ref, o_ref, *, rb, rows):
    jb = pl.program_id(1)
    scale = HEAD_DIM ** -0.5
    nloc = NAT_KR * GRID_W
    grp = NAT_GROUP
    gq = grp * GRID_W

    def scores(g):
        q_all = q_ref[g * gq:(g + 1) * gq, :]
        s_ctx = _dot_nt(q_all, kc_ref[...]) * scale
        starts, s_rows = [], []
        for t in range(grp):
            r = jb * rb + g * grp + t
            rs = jnp.clip(r - NAT_KR // 2, 0, rows - NAT_KR)
            d = r - rs
            start = pl.multiple_of(rs * GRID_W, GRID_W)
            starts.append(start)
            kw = k_ref[pl.ds(start, nloc), :]
            bias = jnp.concatenate([tt_ref[0, 0, NAT_KR - 1 - d + 2 * p] for p in range(NAT_KR // 2)], axis=1)
            s_rows.append(_dot_nt(q_all[t * GRID_W:(t + 1) * GRID_W, :], kw) * scale + bias)
        return jnp.concatenate(s_rows, axis=0), s_ctx, starts

    def finish(g, s_loc, s_ctx, starts):
        mx = jnp.maximum(jnp.max(s_loc, axis=-1, keepdims=True), jnp.max(s_ctx, axis=-1, keepdims=True))
        e_loc = jnp.exp(s_loc - mx)
        e_ctx = jnp.exp(s_ctx - mx)
        den = jnp.sum(e_loc, axis=-1, keepdims=True) + jnp.sum(e_ctx, axis=-1, keepdims=True)
        e_loc = e_loc.astype(BF16)
        o_ctx = _dot(e_ctx.astype(BF16), vc_ref[...])
        o_rows = [_dot(e_loc[t * GRID_W:(t + 1) * GRID_W, :], v_ref[pl.ds(starts[t], nloc), :]) for t in range(grp)]
        o_ref[g * gq:(g + 1) * gq, :] = ((jnp.concatenate(o_rows, axis=0) + o_ctx) / den).astype(BF16)

    ngrp = rb // grp
    pending = scores(0)
    for g in range(ngrp):
        nxt = scores(g + 1) if g + 1 < ngrp else None
        finish(g, *pending)
        pending = nxt


NAT_GROUP = 4
NAT_ROWS = 32


def _nat(pb, pbc, tt, l):
    length = pb.shape[0]
    lc = pbc.shape[0]
    rows = length // GRID_W
    rb = min(NAT_ROWS, rows)
    npair = 2 * NAT_KR - 2
    return pl.pallas_call(
        functools.partial(_nat_kernel, rb=rb, rows=rows),
        grid=(N_NAT, rows // rb),
        in_specs=[pl.BlockSpec((rb * GRID_W, LANES), lambda h, j: (j, PB_NQ // LANES + h)),
                  pl.BlockSpec((length, LANES), lambda h, j: (0, PB_NK // LANES + h)),
                  pl.BlockSpec((length, LANES), lambda h, j: (0, PB_NV // LANES + h)),
                  pl.BlockSpec((lc, LANES), lambda h, j: (0, PB_NK // LANES + h)),
                  pl.BlockSpec((lc, LANES), lambda h, j: (0, PB_NV // LANES + h)),
                  pl.BlockSpec((1, 1, npair, GRID_W, 2 * GRID_W), lambda h, j: (l, h, 0, 0, 0))],
        out_specs=pl.BlockSpec((rb * GRID_W, LANES), lambda h, j: (j, h)),
        out_shape=jax.ShapeDtypeStruct((length, D_NAT), BF16),
        compiler_params=_cparams(("arbitrary", "arbitrary")),
        name="nat",
    )(pb, pb, pb, pbc, pbc, tt)


def _ctx_attn_kernel(q_ref, k_ref, v_ref, o_ref):
    s = _dot_nt(q_ref[...], k_ref[...]) * (HEAD_DIM ** -0.5)
    e = jnp.exp(s - jnp.max(s, axis=-1, keepdims=True))
    den = jnp.sum(e, axis=-1, keepdims=True)
    o_ref[...] = (_dot(e.astype(BF16), v_ref[...]) / den).astype(BF16)


def _ctx_attn(pbc):
    lc = pbc.shape[0]
    return pl.pallas_call(
        _ctx_attn_kernel,
        grid=(N_NAT,),
        in_specs=[pl.BlockSpec((lc, LANES), lambda h: (0, PB_NQ // LANES + h)),
                  pl.BlockSpec((lc, LANES), lambda h: (0, PB_NK // LANES + h)),
                  pl.BlockSpec((lc, LANES), lambda h: (0, PB_NV // LANES + h))],
        out_specs=pl.BlockSpec((lc, LANES), lambda h: (0, h)),
        out_shape=jax.ShapeDtypeStruct((lc, D_NAT), BF16),
        compiler_params=_cparams(("arbitrary",)),
        name="ctx_attn",
    )(pbc, pbc, pbc)


def _mm_post_kernel(*refs, splits, ni, n_total, ncast):
    na = len(splits)
    a_refs = refs[:na]
    w_ref, x_ref, gain_ref, gate_ref = refs[na:na + 4]
    cast_src = refs[na + 4:na + 4 + ncast]
    o_ref = refs[na + 4 + ncast]
    cast_dst = refs[na + 5 + ncast:na + 5 + 2 * ncast]
    acc_ref, ss_ref = refs[na + 5 + 2 * ncast:]
    _cast_plain(cast_src, cast_dst)
    i = pl.program_id(0)
    j = pl.program_id(1)
    slot = i % 2

    @pl.when(i < ni)
    def _():
        acc = None
        off = 0
        for a_ref, kk in zip(a_refs, splits):
            part = _dot(a_ref[...], w_ref[off:off + kk, :])
            acc = part if acc is None else acc + part
            off += kk
        acc_ref[slot, j] = acc
        sq = jnp.sum(acc * acc, axis=-1, keepdims=True)
        ss_ref[slot] = jnp.where(j == 0, sq, ss_ref[slot] + sq)

    @pl.when(i > 0)
    def _():
        prev = 1 - slot
        rstd = lax.rsqrt(ss_ref[prev] * (1.0 / n_total) + EPS)
        o_ref[...] = x_ref[...] + gate_ref[...] * (acc_ref[prev, j] * rstd * gain_ref[...])


def _mm_post(a_list, w, x, gain, gate, cast=()):
    m = x.shape[0]
    splits = tuple(a.shape[1] for a in a_list)
    k, n = w.shape
    bm = min(512, m)
    bn = 1024 if k <= 4096 else 512
    ni = m // bm
    nj = n // bn

    def prev_tile(i, j):
        return (jnp.maximum(i - 1, 0), jnp.where(i == 0, 0, j))

    c_in, c_out, c_shapes = _cast_specs(cast, [(s.shape[2],) for s, _ in cast], nj, (ni + 1) * nj)
    in_specs = [pl.BlockSpec((bm, kk), lambda i, j: (jnp.minimum(i, ni - 1), 0)) for kk in splits]
    in_specs += [pl.BlockSpec((k, bn), lambda i, j: (0, jnp.where(i == ni, nj - 1, j))),
                 pl.BlockSpec((bm, bn), prev_tile),
                 pl.BlockSpec((1, bn), lambda i, j: (0, j)),
                 pl.BlockSpec((1, bn), lambda i, j: (0, j))] + c_in
    res = pl.pallas_call(
        functools.partial(_mm_post_kernel, splits=splits, ni=ni, n_total=n, ncast=len(cast)),
        grid=(ni + 1, nj),
        in_specs=in_specs,
        out_specs=[pl.BlockSpec((bm, bn), prev_tile)] + c_out,
        out_shape=[jax.ShapeDtypeStruct((m, n), F32)] + c_shapes,
        scratch_shapes=[pltpu.VMEM((2, nj, bm, bn), F32), pltpu.VMEM((2, bm, 1), F32)],
        compiler_params=_cparams(("arbitrary", "arbitrary")),
        name="mm_post",
    )(*a_list, w, x, gain, gate, *[s for s, _ in cast])
    return res if cast else res[0]


FFN_HALO = BF16_SUBLANES // 2
FFN_BN = 512


def _ffn_up_kernel(*refs, bm, ni, ncast):
    x_ref, xp_ref, xn_ref, gain_ref, sc_ref, sh_ref, wg_ref, wv_ref, cg_ref, cv_ref = refs[:10]
    cast_src = refs[10:10 + ncast]
    o_ref = refs[10 + ncast]
    cast_dst = refs[11 + ncast:11 + 2 * ncast]
    h_ref = refs[11 + 2 * ncast]
    _cast_plain(cast_src, cast_dst)
    i = pl.program_id(0)
    j = pl.program_id(1)
    hl = FFN_HALO

    @pl.when(j == 0)
    def _():
        h_ref[0:bm, :] = _modulate(x_ref[...], gain_ref[...], sc_ref[...], sh_ref[...]).astype(BF16)

    @pl.when(j == 0)
    def _():
        gain, sc, sh = gain_ref[...], sc_ref[...], sh_ref[...]
        hp = _modulate(xp_ref[...], gain, sc, sh)
        hn = _modulate(xn_ref[...], gain, sc, sh)
        halo = jnp.concatenate([jnp.where(i > 0, hp, 0.0), jnp.where(i < ni - 1, hn, 0.0)], axis=0)
        h_ref[bm:bm + 2 * hl, :] = halo.astype(BF16)

    h = h_ref[...]
    row = lax.broadcasted_iota(jnp.int32, (bm, o_ref.shape[1]), 0)

    def conv(w_ref, cw_ref):
        y = _dot(h, w_ref[...])
        ym = y[0:bm, :]
        y_prev = jnp.where(row == 0, y[bm + hl - 1:bm + hl, :], pltpu.roll(ym, 1, 0))
        y_next = jnp.where(row == bm - 1, y[bm + hl:bm + hl + 1, :], pltpu.roll(ym, bm - 1, 0))
        return y_prev * cw_ref[0:1, :] + ym * cw_ref[1:2, :] + y_next * cw_ref[2:3, :]

    gate = conv(wg_ref, cg_ref)
    val = conv(wv_ref, cv_ref)
    o_ref[...] = (gate * jax.nn.sigmoid(gate) * val).astype(BF16)


def _ffn_up(x, gain, scale, shift, w, cw, cast=()):
    m, d = x.shape
    bm = min(512, m)
    ni = m // bm
    bn = FFN_BN
    nt = D_FF // bn
    hl = FFN_HALO
    per = bm // hl
    nhalo = m // hl
    vec = pl.BlockSpec((1, d), lambda i, j: (0, 0))
    c_in, c_out, c_shapes = _cast_specs(cast, [(a.shape[2],) for a, _ in cast], nt, ni * nt)
    res = pl.pallas_call(
        functools.partial(_ffn_up_kernel, bm=bm, ni=ni, ncast=len(cast)),
        grid=(ni, nt),
        in_specs=[pl.BlockSpec((bm, d), lambda i, j: (i, 0)),
                  pl.BlockSpec((hl, d), lambda i, j: (jnp.maximum(i * per - 1, 0), 0)),
                  pl.BlockSpec((hl, d), lambda i, j: (jnp.minimum((i + 1) * per, nhalo - 1), 0)),
                  vec, vec, vec,
                  pl.BlockSpec((d, bn), lambda i, j: (0, j)),
                  pl.BlockSpec((d, bn), lambda i, j: (0, nt + j)),
                  pl.BlockSpec((3, bn), lambda i, j: (0, j)),
                  pl.BlockSpec((3, bn), lambda i, j: (0, nt + j))] + c_in,
        out_specs=[pl.BlockSpec((bm, bn), lambda i, j: (i, j))] + c_out,
        out_shape=[jax.ShapeDtypeStruct((m, D_FF), BF16)] + c_shapes,
        scratch_shapes=[pltpu.VMEM((bm + 2 * hl, d), BF16)],
        compiler_params=_cparams(("arbitrary", "arbitrary")),
        name="ffn_up",
    )(x, x, x, gain, scale, shift, w, w, cw, cw, *[s for s, _ in cast])
    return res if cast else res[0]


def _prep_w_in_rows(wt_l):
    return jnp.concatenate([wt_l[W_IN_OFFS[t]:W_IN_OFFS[t + 1]] for t in W_IN_ORDER], axis=0).astype(BF16)


def _prep_decay(w_dec, b_dec):
    wd = jnp.zeros((2, LANES, D_GLA_K), F32)
    wd = wd.at[0, 0:GLA_RANK].set(w_dec[0]).at[1, GLA_RANK:2 * GLA_RANK].set(w_dec[1])
    return wd, b_dec.reshape(2, 1, D_GLA_K)


def _state_zero():
    return jnp.zeros((N_GLA // 2, LANES, 2 * HEAD_DIM), F32)


def kernel(x, c, ctx, c_ctx, ada_w, ada_b, norm_mix_pre, norm_mix_post, w_in, gla_w_decay, gla_b_decay,
           gla_norm, fourier_w, nat_rpb, w_out, norm_ffn_pre, norm_ffn_post, ffn_w_up, ffn_w_conv, ffn_w_down):
    depth = ada_w.shape[0]
    d = x.shape[-1]
    xs = x[0]
    cs = ctx[0]
    seq = xs.shape[0]
    lc = cs.shape[0]

    mod = _ada_mod(c, c_ctx, ada_w, ada_b)
    cos_x, sin_x = _rope_tables(seq, True)
    cos_c, sin_c = _rope_tables(lc, False)

    w_in_t = jnp.swapaxes(w_in, 1, 2)
    w_at = w_in_t
    w_cat = _prep_w_in_rows(w_in_t[0])
    side_w_in = (seq // min(512, seq)) * ((N_PB + N_PF) // IN_BN) >= (N_PB + N_PF) // W_IN_CAST_ROWS
    w_out_b = w_out[0].astype(BF16)
    w_up_b = ffn_w_up[0].astype(BF16)
    w_down_b = ffn_w_down[0].astype(BF16)
    tt = _nat_bias_table(nat_rpb)

    for l in range(depth):
        last = l == depth - 1
        mx = [mod[l, 0:1, t * d:(t + 1) * d] for t in range(N_MOD)]
        mc = [mod[l, 1:2, t * d:(t + 1) * d] for t in range(N_MOD)]
        vrow = lambda t: t[l].reshape(1, -1)

        wd2, bd2 = _prep_decay(gla_w_decay[l], gla_b_decay[l])
        wc = _four_weights(fourier_w[l])
        gn = vrow(gla_norm)

        res = _inproj(xs, vrow(norm_mix_pre), mx[1], mx[0], w_cat, w_at, l,
                      cast_src=None if (last or not side_w_in) else (w_in_t, l + 1))
        pbx, pfx, pax = res[:3]
        pbc, pfc, pac = _inproj(cs, vrow(norm_mix_pre), mc[1], mc[0], w_cat, w_at, l)

        gla_c, sc_f, sc_b = _gla_bidir(pfc, pac, pbc, cos_c, sin_c, wd2, bd2, gn, _state_zero(), _state_zero())
        gla_x, _, _ = _gla_bidir(pfx, pax, pbx, cos_x, sin_x, wd2, bd2, gn, sc_f, sc_b)
        four_x = _fourier_mix(pbx, wc)
        nat_x = _nat(pbx, pbc, tt, l)
        xs = _mm_post([four_x, gla_x, nat_x], w_out_b, xs, vrow(norm_mix_post), mx[2])
        if not last:
            four_c = _fourier_mix(pbc, wc)
            nat_c = _ctx_attn(pbc)
            cs = _mm_post([four_c, gla_c, nat_c], w_out_b, cs, vrow(norm_mix_post), mc[2])

        if last:
            act_x = _ffn_up(xs, vrow(norm_ffn_pre), mx[4], mx[3], w_up_b, ffn_w_conv[l])
            xs = _mm_post([act_x], w_down_b, xs, vrow(norm_ffn_post), mx[5])
        else:
            act_x, w_down_n, w_out_n = _ffn_up(xs, vrow(norm_ffn_pre), mx[4], mx[3], w_up_b, ffn_w_conv[l],
                                               cast=((ffn_w_down, l + 1), (w_out, l + 1)))
            xs, w_up_n = _mm_post([act_x], w_down_b, xs, vrow(norm_ffn_post), mx[5], cast=((ffn_w_up, l + 1),))
            act_c = _ffn_up(cs, vrow(norm_ffn_pre), mc[4], mc[3], w_up_b, ffn_w_conv[l])
            cs = _mm_post([act_c], w_down_b, cs, vrow(norm_ffn_post), mc[5])
            w_cat = res[3] if side_w_in else _prep_w_in_rows(w_in_t[l + 1])
            w_out_b, w_up_b, w_down_b = w_out_n, w_up_n, w_down_n
    return xs[None]
```
